```python
import math
import jax
import jax.numpy as jnp
from jax import lax
import numpy as np

D_MODEL = 2048
BATCH = 2
SEQ = 8192
DEPTH = 1
DEC_BATCH = 8
DEC_SEQ = 64
PAST_LEN = 1024

CHUNK = 64
HEAD_DIM = 128
N_HEADS = D_MODEL // (2 * HEAD_DIM)
V_DIM = 2 * HEAD_DIM
ATTN_W = N_HEADS * V_DIM
C_CONV = D_MODEL
CONV_K = 31
N_GROUPS = 4
EXPERTS_PER_GROUP = 4
N_EXPERTS = N_GROUPS * EXPERTS_PER_GROUP
TOP_K = 2
D_EXPERT = D_MODEL // 4
ROPE_THETA = 10000.0
Q_BLOCK = 128
RMS_EPS = 1e-6
LN_EPS = 1e-5
NEG_INF = -1e30
SCALE = HEAD_DIM ** -0.5

Q_END = N_HEADS * 2 * HEAD_DIM
K_END = Q_END + N_HEADS * 2 * HEAD_DIM
V_END = K_END + N_HEADS * V_DIM
GLU_END = V_END + 2 * C_CONV
N_IN = GLU_END + 2 * D_MODEL

kernel_name = 'hybrid_stream_diffattn_conformer_hmoe'

F32 = jnp.float32


def rms_norm(x, g):
    xf = x.astype(F32)
    xf = xf * lax.rsqrt(jnp.mean(xf * xf, axis=-1, keepdims=True) + RMS_EPS)
    return (xf * g.astype(F32)).astype(x.dtype)


def layer_norm(x, g, b):
    xf = x.astype(F32)
    xc = xf - jnp.mean(xf, axis=-1, keepdims=True)
    var = jnp.mean(xc * xc, axis=-1, keepdims=True)
    return (xc * lax.rsqrt(var + LN_EPS) * g.astype(F32) + b.astype(F32)).astype(x.dtype)


def rope(x, pos):
    half = HEAD_DIM // 2
    inv_freq = ROPE_THETA ** (-jnp.arange(half, dtype=F32) / half)
    ang = pos.astype(F32)[:, None] * inv_freq[None, :]
    cos = jnp.cos(ang)[None, :, None, None, :]
    sin = jnp.sin(ang)[None, :, None, None, :]
    xf = x.astype(F32)
    x1, x2 = xf[..., :half], xf[..., half:]
    return jnp.concatenate([x1 * cos - x2 * sin, x2 * cos + x1 * sin], axis=-1).astype(x.dtype)


def diff_attention_block(q, q_pos, k, v, k_pos, lam):
    s = jnp.einsum('bqhmd,bkhmd->bhmqk', q.astype(F32), k.astype(F32)) * SCALE
    visible = (k_pos[None, :] // CHUNK) <= (q_pos[:, None] // CHUNK)
    s = jnp.where(visible[None, None, None], s, NEG_INF)
    p = jax.nn.softmax(s, axis=-1)
    a = p[:, :, 0] - lam * p[:, :, 1]
    return jnp.einsum('bhqk,bkhe->bqhe', a, v.astype(F32))


def diff_attention_prompt(q, k, v, pos, lam):
    b, t = q.shape[0], q.shape[1]
    n_blocks = t // Q_BLOCK
    kf = k.astype(F32)
    vf = v.astype(F32)

    def one_block(i):
        start = i * Q_BLOCK
        qb = lax.dynamic_slice_in_dim(q, start, Q_BLOCK, axis=1)
        qp = lax.dynamic_slice_in_dim(pos, start, Q_BLOCK, axis=0)
        return diff_attention_block(qb, qp, kf, vf, pos, lam)

    o = lax.map(one_block, jnp.arange(n_blocks, dtype=jnp.int32))
    return jnp.moveaxis(o, 0, 1).reshape(b, t, N_HEADS, V_DIM)


def token_mixer(xn, pos, past_k, past_v, past_conv, lam, lam_init, w_in, b_gate, subln_g,
                w_attn_branch, w_dw, b_dw, conv_ln_g, conv_ln_b, w_conv_branch, w_out):
    bx, t, _ = xn.shape
    proj = xn @ w_in
    q = proj[..., :Q_END].reshape(bx, t, N_HEADS, 2, HEAD_DIM)
    k = proj[..., Q_END:K_END].reshape(bx, t, N_HEADS, 2, HEAD_DIM)
    v = proj[..., K_END:V_END].reshape(bx, t, N_HEADS, V_DIM)
    glu_in = proj[..., V_END:GLU_END]
    gate_logits = proj[..., GLU_END:] + b_gate
    q = rope(q, pos)
    k = rope(k, pos)

    if past_k is None:
        o = diff_attention_prompt(q, k, v, pos, lam)
        buf = jnp.zeros((bx, CONV_K - 1, C_CONV), dtype=glu_in.dtype)
    else:
        k_all = jnp.concatenate([past_k.astype(k.dtype), k], axis=1)
        v_all = jnp.concatenate([past_v.astype(v.dtype), v], axis=1)
        k_pos = jnp.arange(past_k.shape[1] + t, dtype=jnp.int32)
        o = diff_attention_block(q, pos, k_all, v_all, k_pos, lam)
        buf = past_conv.astype(glu_in.dtype)
    o = rms_norm(o, subln_g) * (1.0 - lam_init)
    attn_out = o.astype(xn.dtype).reshape(bx, t, ATTN_W) @ w_attn_branch

    glu = glu_in[..., :C_CONV] * jax.nn.sigmoid(glu_in[..., C_CONV:])
    padded = jnp.concatenate([buf, glu], axis=1)
    new_conv = padded[:, -(CONV_K - 1):]
    dw = lax.conv_general_dilated(padded, w_dw[:, None, :].astype(padded.dtype), window_strides=(1,),
                                  padding='VALID', dimension_numbers=('NWC', 'WIO', 'NWC'),
                                  feature_group_count=C_CONV) + b_dw
    conv_out = jax.nn.silu(layer_norm(dw, conv_ln_g, conv_ln_b)) @ w_conv_branch

    g = jax.nn.sigmoid(gate_logits).reshape(bx, t, 2, D_MODEL)
    merged = g[..., 0, :] * attn_out + g[..., 1, :] * conv_out
    return merged @ w_out, k, v, new_conv


def hier_moe(x, w_rg, b_rg, w_re, b_re, w_eg, w_eu, w_ed):
    shp = x.shape
    xt = x.reshape(-1, D_MODEL)
    n = xt.shape[0]
    g_logits = (xt @ w_rg + b_rg).astype(F32)
    g_prob = jax.nn.softmax(g_logits, axis=-1)
    grp = jnp.argmax(g_logits, axis=-1)
    p_grp = jnp.take_along_axis(g_prob, grp[:, None], axis=-1)
    e_logits = (xt @ w_re + b_re).astype(F32).reshape(n, N_GROUPS, EXPERTS_PER_GROUP)
    e_in = jnp.take_along_axis(e_logits, grp[:, None, None], axis=1)[:, 0]
    top_l, top_i = lax.top_k(e_in, TOP_K)
    top_w = jax.nn.softmax(top_l, axis=-1) * p_grp
    expert_id = grp[:, None] * EXPERTS_PER_GROUP + top_i
    combine = jnp.sum(jax.nn.one_hot(expert_id, N_EXPERTS, dtype=F32) * top_w[..., None], axis=1)
    h = jax.nn.silu(jnp.einsum('nd,edf->nef', xt, w_eg)) * jnp.einsum('nd,edf->nef', xt, w_eu)
    h = h * combine.astype(h.dtype)[..., None]
    y = jnp.einsum('nef,efd->nd', h, w_ed)
    return y.reshape(shp)


def setup_inputs(seed: int = 0) -> dict:
    key = jax.random.key(seed)
    ks = jax.random.split(key, 32)

    def nrm(k, shape, scale):
        return jax.random.normal(k, shape, F32) * scale

    def gain(k, shape):
        return 1.0 + 0.01 * jax.random.normal(k, shape, F32)

    return {
        'x_prompt': nrm(ks[0], (BATCH, SEQ, D_MODEL), 1.0),
        'x_sample': nrm(ks[1], (DEC_BATCH, DEC_SEQ, D_MODEL), 1.0),
        'cache_k': nrm(ks[2], (DEPTH, DEC_BATCH, PAST_LEN, N_HEADS, 2, HEAD_DIM), 1.0),
        'cache_v': nrm(ks[3], (DEPTH, DEC_BATCH, PAST_LEN, N_HEADS, V_DIM), 1.0),
        'state_conv': nrm(ks[4], (DEPTH, DEC_BATCH, CONV_K - 1, C_CONV), 0.5),
        'norm_mix_g': gain(ks[5], (DEPTH, D_MODEL)),
        'w_in': nrm(ks[6], (DEPTH, D_MODEL, N_IN), D_MODEL ** -0.5),
        'b_gate': nrm(ks[7], (DEPTH, 2 * D_MODEL), 0.02),
        'lambda_q': nrm(ks[8], (DEPTH, 2, HEAD_DIM), 0.1),
        'lambda_k': nrm(ks[9], (DEPTH, 2, HEAD_DIM), 0.1),
        'subln_g': gain(ks[10], (DEPTH, V_DIM)),
        'w_attn_branch': nrm(ks[11], (DEPTH, ATTN_W, D_MODEL), ATTN_W ** -0.5),
        'w_dw': nrm(ks[12], (DEPTH, CONV_K, C_CONV), CONV_K ** -0.5),
        'b_dw': nrm(ks[13], (DEPTH, C_CONV), 0.02),
        'conv_ln_g': gain(ks[14], (DEPTH, C_CONV)),
        'conv_ln_b': nrm(ks[15], (DEPTH, C_CONV), 0.02),
        'w_conv_branch': nrm(ks[16], (DEPTH, C_CONV, D_MODEL), C_CONV ** -0.5),
        'w_out': nrm(ks[17], (DEPTH, D_MODEL, D_MODEL), D_MODEL ** -0.5),
        'norm_ffn_g': gain(ks[18], (DEPTH, D_MODEL)),
        'w_router_group': nrm(ks[19], (DEPTH, D_MODEL, N_GROUPS), D_MODEL ** -0.5),
        'b_router_group': nrm(ks[20], (DEPTH, N_GROUPS), 0.01),
        'w_router_expert': nrm(ks[21], (DEPTH, D_MODEL, N_EXPERTS), D_MODEL ** -0.5),
        'b_router_expert': nrm(ks[22], (DEPTH, N_EXPERTS), 0.01),
        'w_exp_gate': nrm(ks[23], (DEPTH, N_EXPERTS, D_MODEL, D_EXPERT), D_MODEL ** -0.5),
        'w_exp_up': nrm(ks[24], (DEPTH, N_EXPERTS, D_MODEL, D_EXPERT), D_MODEL ** -0.5),
        'w_exp_down': nrm(ks[25], (DEPTH, N_EXPERTS, D_EXPERT, D_MODEL), D_EXPERT ** -0.5),
        'norm_final_g': gain(ks[26], (D_MODEL,)),
    }


def reference(x_prompt, x_sample, cache_k, cache_v, state_conv, norm_mix_g, w_in, b_gate, lambda_q,
              lambda_k, subln_g, w_attn_branch, w_dw, b_dw, conv_ln_g, conv_ln_b, w_conv_branch, w_out,
              norm_ffn_g, w_router_group, b_router_group, w_router_expert, b_router_expert, w_exp_gate,
              w_exp_up, w_exp_down, norm_final_g):
    pos_p = jnp.arange(SEQ, dtype=jnp.int32)
    pos_s = PAST_LEN + jnp.arange(DEC_SEQ, dtype=jnp.int32)
    hp = x_prompt
    hs = x_sample
    kp_list, vp_list, cp_list, ks_list, vs_list, cs_list = [], [], [], [], [], []
    for l in range(DEPTH):
        lam_init = 0.8 - 0.6 * math.exp(-0.3 * l)
        lq = lambda_q[l].astype(F32)
        lk = lambda_k[l].astype(F32)
        lam = jnp.exp(jnp.sum(lq[0] * lk[0])) - jnp.exp(jnp.sum(lq[1] * lk[1])) + lam_init

        mix_p, kp, vp, cp = token_mixer(rms_norm(hp, norm_mix_g[l]), pos_p, None, None, None, lam, lam_init,
                                        w_in[l], b_gate[l], subln_g[l], w_attn_branch[l], w_dw[l], b_dw[l],
                                        conv_ln_g[l], conv_ln_b[l], w_conv_branch[l], w_out[l])
        mix_s, ksn, vsn, csn = token_mixer(rms_norm(hs, norm_mix_g[l]), pos_s, cache_k[l], cache_v[l],
                                           state_conv[l], lam, lam_init, w_in[l], b_gate[l], subln_g[l],
                                           w_attn_branch[l], w_dw[l], b_dw[l], conv_ln_g[l], conv_ln_b[l],
                                           w_conv_branch[l], w_out[l])
        hp = hp + mix_p
        hs = hs + mix_s
        hp = hp + hier_moe(rms_norm(hp, norm_ffn_g[l]), w_router_group[l], b_router_group[l],
                           w_router_expert[l], b_router_expert[l], w_exp_gate[l], w_exp_up[l], w_exp_down[l])
        hs = hs + hier_moe(rms_norm(hs, norm_ffn_g[l]), w_router_group[l], b_router_group[l],
                           w_router_expert[l], b_router_expert[l], w_exp_gate[l], w_exp_up[l], w_exp_down[l])
        kp_list.append(kp)
        vp_list.append(vp)
        cp_list.append(cp)
        ks_list.append(ksn)
        vs_list.append(vsn)
        cs_list.append(csn)

    y_prompt = rms_norm(hp, norm_final_g)
    y_sample = rms_norm(hs, norm_final_g)
    k_prompt = jnp.stack(kp_list, axis=0)
    v_prompt = jnp.stack(vp_list, axis=0)
    conv_prompt = jnp.stack(cp_list, axis=0)
    k_sample = jnp.stack(ks_list, axis=0)
    v_sample = jnp.stack(vs_list, axis=0)
    conv_sample = jnp.stack(cs_list, axis=0)
    return (y_prompt, y_sample, k_prompt, v_prompt, conv_prompt, k_sample, v_sample, conv_sample)
```

```python
import functools
import math

import jax
import jax.numpy as jnp
from jax import lax
from jax.experimental import pallas as pl
from jax.experimental.pallas import tpu as pltpu

F32 = jnp.float32
BF16 = jnp.bfloat16

D_MODEL = 2048
CHUNK = 64
HEAD_DIM = 128
N_HEADS = D_MODEL // (2 * HEAD_DIM)
V_DIM = 2 * HEAD_DIM
C_CONV = D_MODEL
CONV_K = 31
N_GROUPS = 4
EXPERTS_PER_GROUP = 4
N_EXPERTS = N_GROUPS * EXPERTS_PER_GROUP
D_EXPERT = D_MODEL // 4
ROPE_THETA = 10000.0
RMS_EPS = 1e-6
LN_EPS = 1e-5
NEG_INF = -1e30
SCALE = HEAD_DIM ** -0.5

LANES = 128
HIST_ROWS = 32
HIST_OFF = HIST_ROWS - (CONV_K - 1)
ROUTER_LANES = 128
VMEM_LIMIT = 56 * 1024 * 1024


def _params(sem):
    return pltpu.CompilerParams(dimension_semantics=sem, vmem_limit_bytes=VMEM_LIMIT)


def _sigmoid(x):
    return 1.0 / (1.0 + jnp.exp(-x))


def _rmsnorm_kernel(x_ref, g_ref, o_ref):
    x = x_ref[...]
    ms = jnp.mean(x * x, axis=-1, keepdims=True)
    o_ref[...] = (x * lax.rsqrt(ms + RMS_EPS) * g_ref[...]).astype(o_ref.dtype)


def _rmsnorm(x, g, tm):
    m = x.shape[0]
    return pl.pallas_call(
        _rmsnorm_kernel,
        grid=(m // tm,),
        in_specs=[pl.BlockSpec((tm, D_MODEL), lambda i: (i, 0)),
                  pl.BlockSpec((1, D_MODEL), lambda i: (0, 0))],
        out_specs=pl.BlockSpec((tm, D_MODEL), lambda i: (i, 0)),
        out_shape=jax.ShapeDtypeStruct((m, D_MODEL), BF16),
        compiler_params=_params(("parallel",)),
        name="rmsnorm_in",
    )(x, g.reshape(1, D_MODEL))


def _proj_rope_kernel(x_ref, w_ref, cos_ref, sin_ref, *out_refs, scale, rope, tn):
    x = x_ref[...]
    for c in range(tn // V_DIM):
        acc = jnp.dot(x, w_ref[:, c * V_DIM:(c + 1) * V_DIM], preferred_element_type=F32)
        if rope:
            cs = cos_ref[...]
            sn = sin_ref[...]
            halves = []
            for m in range(2):
                a = acc[:, m * HEAD_DIM:(m + 1) * HEAD_DIM]
                halves.append(a * cs + pltpu.roll(a, HEAD_DIM // 2, axis=1) * sn)
            acc = jnp.concatenate(halves, axis=1)
        if scale != 1.0:
            acc = acc * scale
        for o_ref in out_refs:
            o_ref[:, c * V_DIM:(c + 1) * V_DIM] = acc.astype(o_ref.dtype)


def _proj_rope(xn, w_bf, col_block, cos_t, sin_t, *, tm, rope, scale, out_dtypes, name):
    m = xn.shape[0]
    tn = D_MODEL
    t_tiles = cos_t.shape[0] // tm
    kern = functools.partial(_proj_rope_kernel, scale=scale, rope=rope, tn=tn)
    return pl.pallas_call(
        kern,
        grid=(m // tm,),
        in_specs=[pl.BlockSpec((tm, D_MODEL), lambda i: (i, 0)),
                  pl.BlockSpec((D_MODEL, tn), lambda i: (0, col_block)),
                  pl.BlockSpec((tm, HEAD_DIM), lambda i: (i % t_tiles, 0)),
                  pl.BlockSpec((tm, HEAD_DIM), lambda i: (i % t_tiles, 0))],
        out_specs=[pl.BlockSpec((tm, tn), lambda i: (i, 0)) for _ in out_dtypes],
        out_shape=[jax.ShapeDtypeStruct((m, tn), dt) for dt in out_dtypes],
        compiler_params=_params(("parallel",)),
        name=name,
    )(xn, w_bf, cos_t, sin_t)


def _proj_gate_kernel(x_ref, w_ref, b_ref, o_ref, *, tn):
    x = x_ref[...]
    for c in range(tn // V_DIM):
        sl = slice(c * V_DIM, (c + 1) * V_DIM)
        acc = jnp.dot(x, w_ref[:, sl], preferred_element_type=F32) + b_ref[:, sl]
        o_ref[:, sl] = _sigmoid(acc).astype(o_ref.dtype)


def _proj_gate(xn, w_bf, col_block, b_gate, *, tm):
    m = xn.shape[0]
    tn = D_MODEL
    kern = functools.partial(_proj_gate_kernel, tn=tn)
    return pl.pallas_call(
        kern,
        grid=(2, m // tm),
        in_specs=[pl.BlockSpec((tm, D_MODEL), lambda j, i: (i, 0)),
                  pl.BlockSpec((D_MODEL, tn), lambda j, i: (0, col_block + j)),
                  pl.BlockSpec((1, tn), lambda j, i: (0, j))],
        out_specs=pl.BlockSpec((tm, tn), lambda j, i: (i, j)),
        out_shape=jax.ShapeDtypeStruct((m, 2 * D_MODEL), BF16),
        compiler_params=_params(("parallel", "parallel")),
        name="proj_gate",
    )(xn, w_bf, b_gate.reshape(1, 2 * D_MODEL))


def _proj_glu_conv_kernel(x_ref, wa_ref, wb_ref, init_ref, wdw_ref, bdw_ref,
                          dw_ref, state_ref, pad_ref, carry_ref,
                          *, n_seq, seq_len, tiles_per_seq, tn, row_chunk):
    i = pl.program_id(1)
    x = x_ref[...]
    a = jnp.dot(x, wa_ref[...], preferred_element_type=F32)
    b = jnp.dot(x, wb_ref[...], preferred_element_type=F32)
    glu = a * _sigmoid(b)
    for s in range(n_seq):
        if tiles_per_seq == 1:
            pad_ref[s, 0:HIST_ROWS, :] = init_ref[s]
        else:
            first = (i % tiles_per_seq) == 0

            @pl.when(first)
            def _():
                pad_ref[s, 0:HIST_ROWS, :] = init_ref[s]

            @pl.when(jnp.logical_not(first))
            def _():
                pad_ref[s, 0:HIST_ROWS, :] = carry_ref[...]
        pad_ref[s, HIST_ROWS:HIST_ROWS + seq_len, :] = glu[s * seq_len:(s + 1) * seq_len, :]
        if tiles_per_seq != 1:
            carry_ref[...] = pad_ref[s, seq_len:seq_len + HIST_ROWS, :]
        state_ref[s] = pad_ref[s, seq_len + HIST_OFF:seq_len + HIST_ROWS, :]

    n_chunks = seq_len // row_chunk
    for s in range(n_seq):
        for lc in range(tn // LANES):
            ls = slice(lc * LANES, (lc + 1) * LANES)
            bias = bdw_ref[:, ls]
            for rc in range(n_chunks):
                r0 = rc * row_chunk
                acc = jnp.broadcast_to(bias, (row_chunk, LANES))
                for j in range(CONV_K):
                    start = HIST_OFF + j + r0
                    acc = acc + wdw_ref[j:j + 1, ls] * pad_ref[s, start:start + row_chunk, ls]
                dw_ref[s * seq_len + r0:s * seq_len + r0 + row_chunk, ls] = acc.astype(dw_ref.dtype)


def _proj_glu_conv(xn, w_bf, col_block_a, init_state, w_dw, b_dw, *, n_seq, seq_len, tiles_per_seq, tn):
    m = xn.shape[0]
    tm = n_seq * seq_len
    n_sequences = init_state.shape[0]
    nb = D_MODEL // tn
    kern = functools.partial(_proj_glu_conv_kernel, n_seq=n_seq, seq_len=seq_len,
                             tiles_per_seq=tiles_per_seq, tn=tn, row_chunk=8)
    return pl.pallas_call(
        kern,
        grid=(nb, m // tm),
        in_specs=[pl.BlockSpec((tm, D_MODEL), lambda j, i: (i, 0)),
                  pl.BlockSpec((D_MODEL, tn), lambda j, i: (0, col_block_a * nb + j)),
                  pl.BlockSpec((D_MODEL, tn), lambda j, i: (0, (col_block_a + 1) * nb + j)),
                  pl.BlockSpec((n_seq, HIST_ROWS, tn), lambda j, i: (i // tiles_per_seq, 0, j)),
                  pl.BlockSpec((CONV_K, tn), lambda j, i: (0, j)),
                  pl.BlockSpec((1, tn), lambda j, i: (0, j))],
        out_specs=[pl.BlockSpec((tm, tn), lambda j, i: (i, j)),
                   pl.BlockSpec((n_seq, CONV_K - 1, tn), lambda j, i: (i // tiles_per_seq, 0, j))],
        out_shape=[jax.ShapeDtypeStruct((m, C_CONV), BF16),
                   jax.ShapeDtypeStruct((n_sequences, CONV_K - 1, C_CONV), F32)],
        scratch_shapes=[pltpu.VMEM((n_seq, HIST_ROWS + seq_len, tn), F32),
                        pltpu.VMEM((HIST_ROWS, tn), F32)],
        compiler_params=_params(("parallel", "arbitrary")),
        name="proj_glu_conv",
    )(xn, w_bf, w_bf, init_state, w_dw, b_dw.reshape(1, C_CONV))


def _lambda_value(lq_ref, lk_ref, lam_init):
    prod = lq_ref[...] * lk_ref[...]
    d = jnp.sum(prod, axis=1, keepdims=True)
    e = jnp.exp(d)
    return e[0:1, :] - e[1:2, :] + lam_init


def _subln_store(o_ref, acc0, l0, acc1, l1, lam, g_ref, lam_init):
    o = acc0 / l0 - lam * (acc1 / l1)
    ms = jnp.mean(o * o, axis=-1, keepdims=True)
    o = o * lax.rsqrt(ms + RMS_EPS) * g_ref[...] * (1.0 - lam_init)
    o_ref[...] = o.astype(o_ref.dtype)


def _nt_dot(a, b):
    return lax.dot_general(a, b, (((1,), (1,)), ((), ())), preferred_element_type=F32)


def _flash_prompt_kernel(lq_ref, lk_ref, g_ref, q_ref, k_ref, v_ref, o_ref,
                         m_ref, l_ref, acc_ref, *, tq, lam_init):
    qi = pl.program_id(2)
    q = q_ref[...]
    qm = (q[:, :HEAD_DIM], q[:, HEAD_DIM:])
    m_ref[...] = jnp.full(m_ref.shape, NEG_INF, F32)
    l_ref[...] = jnp.zeros(l_ref.shape, F32)
    acc_ref[...] = jnp.zeros(acc_ref.shape, F32)

    def block(kb, visible):
        k0 = pl.multiple_of(kb * tq, tq)
        k = k_ref[pl.ds(k0, tq), :]
        v = v_ref[pl.ds(k0, tq), :]
        for mp in range(2):
            s = _nt_dot(qm[mp], k[:, mp * HEAD_DIM:(mp + 1) * HEAD_DIM])
            if visible is not None:
                s = jnp.where(visible, s, NEG_INF)
            m_old = m_ref[mp]
            m_new = jnp.maximum(m_old, jnp.max(s, axis=-1, keepdims=True))
            alpha = jnp.exp(m_old - m_new)
            p = jnp.exp(s - m_new)
            l_ref[mp] = l_ref[mp] * alpha + jnp.sum(p, axis=-1, keepdims=True)
            acc_ref[mp] = acc_ref[mp] * alpha + jnp.dot(p.astype(BF16), v, preferred_element_type=F32)
            m_ref[mp] = m_new

    def body(kb, carry):
        block(kb, None)
        return carry

    lax.fori_loop(0, qi, body, 0)
    row_chunk = lax.broadcasted_iota(jnp.int32, (tq, tq), 0) // CHUNK
    col_chunk = lax.broadcasted_iota(jnp.int32, (tq, tq), 1) // CHUNK
    block(qi, col_chunk <= row_chunk)

    lam = _lambda_value(lq_ref, lk_ref, lam_init)
    _subln_store(o_ref, acc_ref[0], l_ref[0], acc_ref[1], l_ref[1], lam, g_ref, lam_init)


def _flash_prompt(q, k, v, lam_q, lam_k, subln_g, *, batch, seq, tq, lam_init):
    q3 = q.reshape(batch, seq, D_MODEL)
    k3 = k.reshape(batch, seq, D_MODEL)
    v3 = v.reshape(batch, seq, D_MODEL)
    kern = functools.partial(_flash_prompt_kernel, tq=tq, lam_init=lam_init)
    small = lambda shape: pl.BlockSpec(shape, lambda b, h, i: (0, 0))
    out = pl.pallas_call(
        kern,
        grid=(batch, N_HEADS, seq // tq),
        in_specs=[small((2, HEAD_DIM)), small((2, HEAD_DIM)), small((1, V_DIM)),
                  pl.BlockSpec((None, tq, V_DIM), lambda b, h, i: (b, i, h)),
                  pl.BlockSpec((None, seq, V_DIM), lambda b, h, i: (b, 0, h)),
                  pl.BlockSpec((None, seq, V_DIM), lambda b, h, i: (b, 0, h))],
        out_specs=pl.BlockSpec((None, tq, V_DIM), lambda b, h, i: (b, i, h)),
        out_shape=jax.ShapeDtypeStruct((batch, seq, D_MODEL), BF16),
        scratch_shapes=[pltpu.VMEM((2, tq, 1), F32), pltpu.VMEM((2, tq, 1), F32),
                        pltpu.VMEM((2, tq, V_DIM), F32)],
        compiler_params=_params(("parallel", "parallel", "arbitrary")),
        name="flash_prompt",
    )(lam_q, lam_k, subln_g.reshape(1, V_DIM), q3, k3, v3)
    return out.reshape(batch * seq, D_MODEL)


def _attn_sample_kernel(lq_ref, lk_ref, g_ref, q_ref, kn_ref, vn_ref, kc_ref, vc_ref, o_ref, *, lam_init):
    q = q_ref[...]
    kc = kc_ref[...].astype(BF16)
    vc = vc_ref[...].astype(BF16)
    kn = kn_ref[...]
    vn = vn_ref[...]
    accs, ls = [], []
    for mp in range(2):
        sl = slice(mp * HEAD_DIM, (mp + 1) * HEAD_DIM)
        s_past = _nt_dot(q[:, sl], kc[:, sl])
        s_new = _nt_dot(q[:, sl], kn[:, sl])
        mx = jnp.maximum(jnp.max(s_past, axis=-1, keepdims=True), jnp.max(s_new, axis=-1, keepdims=True))
        p_past = jnp.exp(s_past - mx)
        p_new = jnp.exp(s_new - mx)
        ls.append(jnp.sum(p_past, axis=-1, keepdims=True) + jnp.sum(p_new, axis=-1, keepdims=True))
        accs.append(jnp.dot(p_past.astype(BF16), vc, preferred_element_type=F32)
                    + jnp.dot(p_new.astype(BF16), vn, preferred_element_type=F32))
    lam = _lambda_value(lq_ref, lk_ref, lam_init)
    _subln_store(o_ref, accs[0], ls[0], accs[1], ls[1], lam, g_ref, lam_init)


def _attn_sample(q, k_new, v_new, cache_k, cache_v, lam_q, lam_k, subln_g, *, n_streams, dec_seq, past_len,
                 lam_init):
    kern = functools.partial(_attn_sample_kernel, lam_init=lam_init)
    small = lambda shape: pl.BlockSpec(shape, lambda b, h: (0, 0))
    new = pl.BlockSpec((dec_seq, V_DIM), lambda b, h: (b, h))
    past = pl.BlockSpec((None, past_len, V_DIM), lambda b, h: (b, 0, h))
    return pl.pallas_call(
        kern,
        grid=(n_streams, N_HEADS),
        in_specs=[small((2, HEAD_DIM)), small((2, HEAD_DIM)), small((1, V_DIM)), new, new, new, past, past],
        out_specs=new,
        out_shape=jax.ShapeDtypeStruct((n_streams * dec_seq, D_MODEL), BF16),
        compiler_params=_params(("parallel", "parallel")),
        name="attn_sample",
    )(lam_q, lam_k, subln_g.reshape(1, V_DIM), q, k_new, v_new, cache_k, cache_v)


def _attn_branch_kernel(o_ref, w_ref, g_ref, out_ref):
    o = o_ref[...]
    for c in range(D_MODEL // V_DIM):
        sl = slice(c * V_DIM, (c + 1) * V_DIM)
        acc = jnp.dot(o, w_ref[:, sl], preferred_element_type=F32)
        out_ref[:, sl] = (g_ref[:, sl].astype(F32) * acc).astype(out_ref.dtype)


def _conv_branch_kernel(dw_ref, lng_ref, lnb_ref, w_ref, g_ref, ga_ref, out_ref, act_ref):
    x = dw_ref[...].astype(F32)
    mu = jnp.mean(x, axis=-1, keepdims=True)
    xc = x - mu
    var = jnp.mean(xc * xc, axis=-1, keepdims=True)
    y = xc * lax.rsqrt(var + LN_EPS) * lng_ref[...] + lnb_ref[...]
    act_ref[...] = (y * _sigmoid(y)).astype(BF16)
    for c in range(D_MODEL // V_DIM):
        sl = slice(c * V_DIM, (c + 1) * V_DIM)
        acc = jnp.dot(act_ref[...], w_ref[:, sl], preferred_element_type=F32)
        merged = ga_ref[:, sl].astype(F32) + g_ref[:, sl].astype(F32) * acc
        out_ref[:, sl] = merged.astype(out_ref.dtype)


def _router_combine(logits):
    lane = lax.broadcasted_iota(jnp.int32, logits.shape, 1)
    big = jnp.int32(ROUTER_LANES)
    gmask = lane < N_GROUPS
    gl = jnp.where(gmask, logits, -jnp.inf)
    gmax = jnp.max(gl, axis=-1, keepdims=True)
    grp = jnp.min(jnp.where(gl == gmax, lane, big), axis=-1, keepdims=True)
    denom = jnp.sum(jnp.where(gmask, jnp.exp(gl - gmax), 0.0), axis=-1, keepdims=True)
    p_grp = 1.0 / denom
    eidx = lane - N_GROUPS
    emask = (eidx >= 0) & (eidx < N_EXPERTS) & ((eidx // EXPERTS_PER_GROUP) == grp)
    ev = jnp.where(emask, logits, -jnp.inf)
    l1 = jnp.max(ev, axis=-1, keepdims=True)
    i1 = jnp.min(jnp.where(ev == l1, lane, big), axis=-1, keepdims=True)
    ev2 = jnp.where(lane == i1, -jnp.inf, ev)
    l2 = jnp.max(ev2, axis=-1, keepdims=True)
    i2 = jnp.min(jnp.where(ev2 == l2, lane, big), axis=-1, keepdims=True)
    e2 = jnp.exp(l2 - l1)
    w1 = p_grp / (1.0 + e2)
    w2 = p_grp * e2 / (1.0 + e2)
    return jnp.where(lane == i1, w1, 0.0) + jnp.where(lane == i2, w2, 0.0)


def _out_proj_kernel(mg_ref, w_ref, x_ref, gffn_ref, wr_ref, br_ref, h_ref, hn_ref, comb_ref):
    mg = mg_ref[...]
    for c in range(D_MODEL // V_DIM):
        sl = slice(c * V_DIM, (c + 1) * V_DIM)
        h_ref[:, sl] = x_ref[:, sl] + jnp.dot(mg, w_ref[:, sl], preferred_element_type=F32)
    h = h_ref[...]
    ms = jnp.mean(h * h, axis=-1, keepdims=True)
    hn = h * lax.rsqrt(ms + RMS_EPS) * gffn_ref[...]
    hn_ref[...] = hn.astype(hn_ref.dtype)
    logits = jnp.dot(hn, wr_ref[...], preferred_element_type=F32, precision=lax.Precision.HIGHEST) + br_ref[...]
    comb_ref[...] = _router_combine(logits)


def _token_spec(tm, width=D_MODEL, col=0):
    return pl.BlockSpec((tm, width), lambda i: (i, col))


def _full_spec(shape):
    return pl.BlockSpec(shape, lambda i: (0,) * len(shape))


def _attn_branch(o, w_attn_bf, gates, *, tm):
    m = o.shape[0]
    return pl.pallas_call(
        _attn_branch_kernel,
        grid=(m // tm,),
        in_specs=[_token_spec(tm), _full_spec((ATTN_W, D_MODEL)), _token_spec(tm, D_MODEL, 0)],
        out_specs=_token_spec(tm),
        out_shape=jax.ShapeDtypeStruct((m, D_MODEL), BF16),
        compiler_params=_params(("parallel",)),
        name="attn_branch",
    )(o, w_attn_bf, gates)


def _conv_branch(dw, ln_g, ln_b, w_conv_bf, gates, ga, *, tm):
    m = dw.shape[0]
    return pl.pallas_call(
        _conv_branch_kernel,
        grid=(m // tm,),
        in_specs=[_token_spec(tm), _full_spec((1, C_CONV)), _full_spec((1, C_CONV)),
                  _full_spec((C_CONV, D_MODEL)), _token_spec(tm, D_MODEL, 1), _token_spec(tm)],
        out_specs=_token_spec(tm),
        out_shape=jax.ShapeDtypeStruct((m, D_MODEL), BF16),
        scratch_shapes=[pltpu.VMEM((tm, C_CONV), BF16)],
        compiler_params=_params(("parallel",)),
        name="conv_branch",
    )(dw, ln_g.reshape(1, C_CONV), ln_b.reshape(1, C_CONV), w_conv_bf, gates, ga)


def _out_proj(merged, w_out_bf, x, g_ffn, w_router, b_router, *, tm):
    m = merged.shape[0]
    return pl.pallas_call(
        _out_proj_kernel,
        grid=(m // tm,),
        in_specs=[_token_spec(tm), _full_spec((D_MODEL, D_MODEL)), _token_spec(tm), _full_spec((1, D_MODEL)),
                  _full_spec((D_MODEL, ROUTER_LANES)), _full_spec((1, ROUTER_LANES))],
        out_specs=[_token_spec(tm), _token_spec(tm), _token_spec(tm, ROUTER_LANES)],
        out_shape=[jax.ShapeDtypeStruct((m, D_MODEL), F32), jax.ShapeDtypeStruct((m, D_MODEL), BF16),
                   jax.ShapeDtypeStruct((m, ROUTER_LANES), F32)],
        compiler_params=_params(("parallel",)),
        name="out_proj_router",
    )(merged, w_out_bf, x, g_ffn.reshape(1, D_MODEL), w_router, b_router)


def _moe_dense_kernel(hn_ref, h_ref, comb_ref, wg_ref, wu_ref, wd_ref, gfin_ref, y_ref, acc_ref):
    e = pl.program_id(1)

    @pl.when(e == 0)
    def _():
        acc_ref[...] = h_ref[...]

    x = hn_ref[...]
    gate = jnp.dot(x, wg_ref[...], preferred_element_type=F32)
    up = jnp.dot(x, wu_ref[...], preferred_element_type=F32)
    comb = comb_ref[...]
    lane = lax.broadcasted_iota(jnp.int32, comb.shape, 1)
    c = jnp.sum(jnp.where(lane == e + N_GROUPS, comb, 0.0), axis=-1, keepdims=True)
    hid = (gate * _sigmoid(gate) * up * c).astype(BF16)
    acc_ref[...] += jnp.dot(hid, wd_ref[...], preferred_element_type=F32)

    @pl.when(e == N_EXPERTS - 1)
    def _():
        y = acc_ref[...]
        ms = jnp.mean(y * y, axis=-1, keepdims=True)
        y_ref[...] = y * lax.rsqrt(ms + RMS_EPS) * gfin_ref[...]


def _moe_dense(hn, h, comb, wg_bf, wu_bf, wd_bf, g_final, *, tm):
    m = hn.shape[0]
    return pl.pallas_call(
        _moe_dense_kernel,
        grid=(m // tm, N_EXPERTS),
        in_specs=[pl.BlockSpec((tm, D_MODEL), lambda i, e: (i, 0)),
                  pl.BlockSpec((tm, D_MODEL), lambda i, e: (i, 0)),
                  pl.BlockSpec((tm, ROUTER_LANES), lambda i, e: (i, 0)),
                  pl.BlockSpec((None, D_MODEL, D_EXPERT), lambda i, e: (e, 0, 0)),
                  pl.BlockSpec((None, D_MODEL, D_EXPERT), lambda i, e: (e, 0, 0)),
                  pl.BlockSpec((None, D_EXPERT, D_MODEL), lambda i, e: (e, 0, 0)),
                  pl.BlockSpec((1, D_MODEL), lambda i, e: (0, 0))],
        out_specs=pl.BlockSpec((tm, D_MODEL), lambda i, e: (i, 0)),
        out_shape=jax.ShapeDtypeStruct((m, D_MODEL), F32),
        scratch_shapes=[pltpu.VMEM((tm, D_MODEL), F32)],
        compiler_params=_params(("parallel", "arbitrary")),
        name="moe_dense",
    )(hn, h, comb, wg_bf, wu_bf, wd_bf, g_final.reshape(1, D_MODEL))


ATTN_W = N_HEADS * V_DIM


def _rope_tables(pos):
    half = HEAD_DIM // 2
    inv_freq = ROPE_THETA ** (-jnp.arange(half, dtype=F32) / half)
    ang = pos.astype(F32)[:, None] * inv_freq[None, :]
    cos, sin = jnp.cos(ang), jnp.sin(ang)
    return jnp.concatenate([cos, cos], axis=-1), jnp.concatenate([-sin, sin], axis=-1)


def _layer_tokens(x, pos_tables, init_state, attn_fn, wts, *, tm, n_seq, seq_len, tiles_per_seq, glu_tn):
    cos_t, sin_t = pos_tables
    xn = _rmsnorm(x, wts["norm_mix_g"], tm)
    (q,) = _proj_rope(xn, wts["w_in"], 0, cos_t, sin_t, tm=tm, rope=True, scale=SCALE,
                      out_dtypes=(BF16,), name="proj_q")
    k_f32, k_bf = _proj_rope(xn, wts["w_in"], 1, cos_t, sin_t, tm=tm, rope=True, scale=1.0,
                             out_dtypes=(F32, BF16), name="proj_k")
    v_f32, v_bf = _proj_rope(xn, wts["w_in"], 2, cos_t, sin_t, tm=tm, rope=False, scale=1.0,
                             out_dtypes=(F32, BF16), name="proj_v")
    dw, new_conv = _proj_glu_conv(xn, wts["w_in"], 3, init_state, wts["w_dw"], wts["b_dw"],
                                  n_seq=n_seq, seq_len=seq_len, tiles_per_seq=tiles_per_seq, tn=glu_tn)
    gates = _proj_gate(xn, wts["w_in"], 5, wts["b_gate"], tm=tm)
    o = attn_fn(q, k_bf, v_bf)
    ga = _attn_branch(o, wts["w_attn"], gates, tm=tm)
    merged = _conv_branch(dw, wts["conv_ln_g"], wts["conv_ln_b"], wts["w_conv"], gates, ga, tm=tm)
    h, hn, comb = _out_proj(merged, wts["w_out"], x, wts["norm_ffn_g"], wts["w_router"], wts["b_router"], tm=tm)
    y = _moe_dense(hn, h, comb, wts["w_eg"], wts["w_eu"], wts["w_ed"], wts["norm_final_g"], tm=tm)
    return y, k_f32, v_f32, new_conv


def kernel(x_prompt, x_sample, cache_k, cache_v, state_conv, norm_mix_g, w_in, b_gate, lambda_q, lambda_k, subln_g, w_attn_branch, w_dw, b_dw, conv_ln_g, conv_ln_b, w_conv_branch, w_out, norm_ffn_g, w_router_group, b_router_group, w_router_expert, b_router_expert, w_exp_gate, w_exp_up, w_exp_down, norm_final_g):
    batch, seq, _ = x_prompt.shape
    n_streams, dec_seq, _ = x_sample.shape
    depth, _, past_len = cache_k.shape[:3]
    assert depth == 1
    assert past_len % CHUNK == 0 and dec_seq <= CHUNK
    l = 0
    lam_init = 0.8 - 0.6 * math.exp(-0.3 * l)

    pad_r = ROUTER_LANES - N_GROUPS - N_EXPERTS
    w_router = jnp.concatenate([w_router_group[l], w_router_expert[l], jnp.zeros((D_MODEL, pad_r), F32)], axis=1)
    b_router = jnp.concatenate([b_router_group[l], b_router_expert[l], jnp.zeros((pad_r,), F32)]).reshape(1, -1)
    wts = dict(
        norm_mix_g=norm_mix_g[l], w_in=w_in[l].astype(BF16), b_gate=b_gate[l],
        w_attn=w_attn_branch[l].astype(BF16), w_dw=w_dw[l], b_dw=b_dw[l],
        conv_ln_g=conv_ln_g[l], conv_ln_b=conv_ln_b[l], w_conv=w_conv_branch[l].astype(BF16),
        w_out=w_out[l].astype(BF16), norm_ffn_g=norm_ffn_g[l], w_router=w_router, b_router=b_router,
        w_eg=w_exp_gate[l].astype(BF16), w_eu=w_exp_up[l].astype(BF16), w_ed=w_exp_down[l].astype(BF16),
        norm_final_g=norm_final_g)
    lam_q, lam_k, sub_g = lambda_q[l], lambda_k[l], subln_g[l]

    tm = 512
    tabs_p = _rope_tables(jnp.arange(seq, dtype=jnp.int32))
    init_p = jnp.zeros((batch, HIST_ROWS, C_CONV), F32)
    attn_p = functools.partial(_flash_prompt, lam_q=lam_q, lam_k=lam_k, subln_g=sub_g,
                               batch=batch, seq=seq, tq=tm, lam_init=lam_init)
    y_p, k_p, v_p, conv_p = _layer_tokens(
        x_prompt.reshape(batch * seq, D_MODEL), tabs_p, init_p, attn_p, wts,
        tm=tm, n_seq=1, seq_len=tm, tiles_per_seq=seq // tm, glu_tn=512)

    cos_s, sin_s = _rope_tables(past_len + jnp.arange(dec_seq, dtype=jnp.int32))
    tabs_s = (jnp.tile(cos_s, (n_streams, 1)), jnp.tile(sin_s, (n_streams, 1)))
    init_s = jnp.pad(state_conv[l], ((0, 0), (HIST_OFF, 0), (0, 0)))
    ck = cache_k[l].reshape(n_streams, past_len, D_MODEL)
    cv = cache_v[l].reshape(n_streams, past_len, D_MODEL)
    attn_s = functools.partial(_attn_sample, cache_k=ck, cache_v=cv, lam_q=lam_q, lam_k=lam_k, subln_g=sub_g,
                               n_streams=n_streams, dec_seq=dec_seq, past_len=past_len, lam_init=lam_init)
    m_s = n_streams * dec_seq
    y_s, k_s, v_s, conv_s = _layer_tokens(
        x_sample.reshape(m_s, D_MODEL), tabs_s, init_s, attn_s, wts,
        tm=m_s, n_seq=n_streams, seq_len=dec_seq, tiles_per_seq=1, glu_tn=512)

    return (y_p.reshape(batch, seq, D_MODEL),
            y_s.reshape(n_streams, dec_seq, D_MODEL),
            k_p.reshape(1, batch, seq, N_HEADS, 2, HEAD_DIM),
            v_p.reshape(1, batch, seq, N_HEADS, V_DIM),
            conv_p.reshape(1, batch, CONV_K - 1, C_CONV),
            k_s.reshape(1, n_streams, dec_seq, N_HEADS, 2, HEAD_DIM),
            v_s.reshape(1, n_streams, dec_seq, N_HEADS, V_DIM),
            conv_s.reshape(1, n_streams, CONV_K - 1, C_CONV))
```

```python
import functools
import math

import jax
import jax.numpy as jnp
from jax import lax
from jax.experimental import pallas as pl
from jax.experimental.pallas import tpu as pltpu

F32 = jnp.float32
BF16 = jnp.bfloat16

D_MODEL = 2048
CHUNK = 64
HEAD_DIM = 128
N_HEADS = D_MODEL // (2 * HEAD_DIM)
V_DIM = 2 * HEAD_DIM
C_CONV = D_MODEL
CONV_K = 31
N_GROUPS = 4
EXPERTS_PER_GROUP = 4
N_EXPERTS = N_GROUPS * EXPERTS_PER_GROUP
D_EXPERT = D_MODEL // 4
ROPE_THETA = 10000.0
RMS_EPS = 1e-6
LN_EPS = 1e-5
NEG_INF = -1e30
SCALE = HEAD_DIM ** -0.5
LOG2_E = math.log2(math.e)
Q_STRIP = 256

LANES = 128
HIST_ROWS = 32
HIST_OFF = HIST_ROWS - (CONV_K - 1)
ROUTER_LANES = 128
VMEM_LIMIT = 56 * 1024 * 1024


def _params(sem):
    return pltpu.CompilerParams(dimension_semantics=sem, vmem_limit_bytes=VMEM_LIMIT)


def _sigmoid(x):
    return 1.0 / (1.0 + jnp.exp(-x))


def _rmsnorm_kernel(x_ref, g_ref, o_ref):
    x = x_ref[...]
    ms = jnp.mean(x * x, axis=-1, keepdims=True)
    o_ref[...] = (x * lax.rsqrt(ms + RMS_EPS) * g_ref[...]).astype(o_ref.dtype)


def _rmsnorm(x, g, tm):
    m = x.shape[0]
    return pl.pallas_call(
        _rmsnorm_kernel,
        grid=(m // tm,),
        in_specs=[pl.BlockSpec((tm, D_MODEL), lambda i: (i, 0)),
                  pl.BlockSpec((1, D_MODEL), lambda i: (0, 0))],
        out_specs=pl.BlockSpec((tm, D_MODEL), lambda i: (i, 0)),
        out_shape=jax.ShapeDtypeStruct((m, D_MODEL), BF16),
        compiler_params=_params(("parallel",)),
        name="rmsnorm_in",
    )(x, g.reshape(1, D_MODEL))


def _proj_rope_kernel(x_ref, w_ref, cos_ref, sin_ref, *out_refs, scale, rope, tn, out_kinds):
    x = x_ref[...]
    for c in range(tn // V_DIM):
        acc = jnp.dot(x, w_ref[:, c * V_DIM:(c + 1) * V_DIM], preferred_element_type=F32)
        if rope:
            cs = cos_ref[...]
            sn = sin_ref[...]
            halves = []
            for m in range(2):
                a = acc[:, m * HEAD_DIM:(m + 1) * HEAD_DIM]
                halves.append(a * cs + pltpu.roll(a, HEAD_DIM // 2, axis=1) * sn)
            acc = jnp.concatenate(halves, axis=1)
        if scale != 1.0:
            acc = acc * scale
        for o_ref, kind in zip(out_refs, out_kinds):
            if kind == "head_transposed":
                o_ref[c] = acc.T.astype(o_ref.dtype)
            else:
                o_ref[:, c * V_DIM:(c + 1) * V_DIM] = acc.astype(o_ref.dtype)


def _proj_rope(xn, w_bf, col_block, cos_t, sin_t, *, tm, rope, scale, outs, name):
    m = xn.shape[0]
    tn = D_MODEL
    t_tiles = cos_t.shape[0] // tm
    kinds = tuple(k for k, _ in outs)
    kern = functools.partial(_proj_rope_kernel, scale=scale, rope=rope, tn=tn, out_kinds=kinds)
    out_specs, out_shape = [], []
    for kind, dt in outs:
        if kind == "head_transposed":
            out_specs.append(pl.BlockSpec((N_HEADS, None, V_DIM, tm), lambda i: (0, i, 0, 0)))
            out_shape.append(jax.ShapeDtypeStruct((N_HEADS, m // tm, V_DIM, tm), dt))
        else:
            out_specs.append(pl.BlockSpec((tm, tn), lambda i: (i, 0)))
            out_shape.append(jax.ShapeDtypeStruct((m, tn), dt))
    return pl.pallas_call(
        kern,
        grid=(m // tm,),
        in_specs=[pl.BlockSpec((tm, D_MODEL), lambda i: (i, 0)),
                  pl.BlockSpec((D_MODEL, tn), lambda i: (0, col_block)),
                  pl.BlockSpec((tm, HEAD_DIM), lambda i: (i % t_tiles, 0)),
                  pl.BlockSpec((tm, HEAD_DIM), lambda i: (i % t_tiles, 0))],
        out_specs=out_specs,
        out_shape=out_shape,
        compiler_params=_params(("parallel",)),
        name=name,
    )(xn, w_bf, cos_t, sin_t)


def _proj_gate_kernel(x_ref, w_ref, b_ref, o_ref, *, tn):
    x = x_ref[...]
    for c in range(tn // V_DIM):
        sl = slice(c * V_DIM, (c + 1) * V_DIM)
        acc = jnp.dot(x, w_ref[:, sl], preferred_element_type=F32) + b_ref[:, sl]
        o_ref[:, sl] = _sigmoid(acc).astype(o_ref.dtype)


def _proj_gate(xn, w_bf, col_block, b_gate, *, tm):
    m = xn.shape[0]
    tn = D_MODEL
    kern = functools.partial(_proj_gate_kernel, tn=tn)
    return pl.pallas_call(
        kern,
        grid=(2, m // tm),
        in_specs=[pl.BlockSpec((tm, D_MODEL), lambda j, i: (i, 0)),
                  pl.BlockSpec((D_MODEL, tn), lambda j, i: (0, col_block + j)),
                  pl.BlockSpec((1, tn), lambda j, i: (0, j))],
        out_specs=pl.BlockSpec((tm, tn), lambda j, i: (i, j)),
        out_shape=jax.ShapeDtypeStruct((m, 2 * D_MODEL), BF16),
        compiler_params=_params(("parallel", "parallel")),
        name="proj_gate",
    )(xn, w_bf, b_gate.reshape(1, 2 * D_MODEL))


def _proj_glu_conv_kernel(x_ref, wa_ref, wb_ref, init_ref, wdw_ref, bdw_ref,
                          dw_ref, state_ref, pad_ref, carry_ref,
                          *, n_seq, seq_len, tiles_per_seq, tn, row_chunk):
    i = pl.program_id(1)
    x = x_ref[...]
    a = jnp.dot(x, wa_ref[...], preferred_element_type=F32)
    b = jnp.dot(x, wb_ref[...], preferred_element_type=F32)
    glu = a * _sigmoid(b)
    for s in range(n_seq):
        if tiles_per_seq == 1:
            pad_ref[s, 0:HIST_ROWS, :] = init_ref[s]
        else:
            first = (i % tiles_per_seq) == 0

            @pl.when(first)
            def _():
                pad_ref[s, 0:HIST_ROWS, :] = init_ref[s]

            @pl.when(jnp.logical_not(first))
            def _():
                pad_ref[s, 0:HIST_ROWS, :] = carry_ref[...]
        pad_ref[s, HIST_ROWS:HIST_ROWS + seq_len, :] = glu[s * seq_len:(s + 1) * seq_len, :]
        if tiles_per_seq != 1:
            carry_ref[...] = pad_ref[s, seq_len:seq_len + HIST_ROWS, :]
        state_ref[s] = pad_ref[s, seq_len + HIST_OFF:seq_len + HIST_ROWS, :]

    n_chunks = seq_len // row_chunk
    for s in range(n_seq):
        for lc in range(tn // LANES):
            ls = slice(lc * LANES, (lc + 1) * LANES)
            bias = bdw_ref[:, ls]
            for rc in range(n_chunks):
                r0 = rc * row_chunk
                acc = jnp.broadcast_to(bias, (row_chunk, LANES))
                for j in range(CONV_K):
                    start = HIST_OFF + j + r0
                    acc = acc + wdw_ref[j:j + 1, ls] * pad_ref[s, start:start + row_chunk, ls]
                dw_ref[s * seq_len + r0:s * seq_len + r0 + row_chunk, ls] = acc.astype(dw_ref.dtype)


def _proj_glu_conv(xn, w_bf, col_block_a, init_state, w_dw, b_dw, *, n_seq, seq_len, tiles_per_seq, tn):
    m = xn.shape[0]
    tm = n_seq * seq_len
    n_sequences = init_state.shape[0]
    nb = D_MODEL // tn
    kern = functools.partial(_proj_glu_conv_kernel, n_seq=n_seq, seq_len=seq_len,
                             tiles_per_seq=tiles_per_seq, tn=tn, row_chunk=8)
    return pl.pallas_call(
        kern,
        grid=(nb, m // tm),
        in_specs=[pl.BlockSpec((tm, D_MODEL), lambda j, i: (i, 0)),
                  pl.BlockSpec((D_MODEL, tn), lambda j, i: (0, col_block_a * nb + j)),
                  pl.BlockSpec((D_MODEL, tn), lambda j, i: (0, (col_block_a + 1) * nb + j)),
                  pl.BlockSpec((n_seq, HIST_ROWS, tn), lambda j, i: (i // tiles_per_seq, 0, j)),
                  pl.BlockSpec((CONV_K, tn), lambda j, i: (0, j)),
                  pl.BlockSpec((1, tn), lambda j, i: (0, j))],
        out_specs=[pl.BlockSpec((tm, tn), lambda j, i: (i, j)),
                   pl.BlockSpec((n_seq, CONV_K - 1, tn), lambda j, i: (i // tiles_per_seq, 0, j))],
        out_shape=[jax.ShapeDtypeStruct((m, C_CONV), BF16),
                   jax.ShapeDtypeStruct((n_sequences, CONV_K - 1, C_CONV), F32)],
        scratch_shapes=[pltpu.VMEM((n_seq, HIST_ROWS + seq_len, tn), F32),
                        pltpu.VMEM((HIST_ROWS, tn), F32)],
        compiler_params=_params(("parallel", "arbitrary")),
        name="proj_glu_conv",
    )(xn, w_bf, w_bf, init_state, w_dw, b_dw.reshape(1, C_CONV))


def _lambda_value(lq_ref, lk_ref, lam_init):
    prod = lq_ref[...] * lk_ref[...]
    d = jnp.sum(prod, axis=1, keepdims=True)
    e = jnp.exp(d)
    return e[0:1, :] - e[1:2, :] + lam_init


def _subln_store(o_ref, acc0, l0, acc1, l1, lam, g_ref, lam_init):
    o = acc0 / l0 - lam * (acc1 / l1)
    ms = jnp.mean(o * o, axis=-1, keepdims=True)
    o = o * lax.rsqrt(ms + RMS_EPS) * g_ref[...] * (1.0 - lam_init)
    o_ref[...] = o.astype(o_ref.dtype)


def _nt_dot(a, b):
    return lax.dot_general(a, b, (((1,), (1,)), ((), ())), preferred_element_type=F32)


def _flash_prompt_kernel(lq_ref, lk_ref, g_ref, q_ref, k_ref, vt_ref, o_ref,
                         m_ref, l_ref, acc_ref, st_ref, p_ref, alpha_ref, *, tq, tk, lam_init):
    nd = tq // tk
    assert nd == 2
    qi = pl.program_id(2)
    q = q_ref[...]
    m_ref[...] = jnp.full(m_ref.shape, NEG_INF, F32)
    l_ref[...] = jnp.zeros(l_ref.shape, F32)
    acc_ref[...] = jnp.zeros(acc_ref.shape, F32)
    units = [(mp, c) for c in range(tq // Q_STRIP) for mp in range(2)]

    def scores(k, slot, diag, mp, c):
        sl = slice(mp * HEAD_DIM, (mp + 1) * HEAD_DIM)
        cs = slice(c * Q_STRIP, (c + 1) * Q_STRIP)
        st = _nt_dot(k[:, sl], q[cs, sl])
        if diag is not None:
            key_chunk = (lax.broadcasted_iota(jnp.int32, (tk, Q_STRIP), 0) + diag * tk) // CHUNK
            qry_chunk = (lax.broadcasted_iota(jnp.int32, (tk, Q_STRIP), 1) + c * Q_STRIP) // CHUNK
            st = jnp.where(key_chunk <= qry_chunk, st, NEG_INF)
        st_ref[slot, mp, :, cs] = st

    def softmax(slot, mp, c):
        cs = slice(c * Q_STRIP, (c + 1) * Q_STRIP)
        st = st_ref[slot, mp, :, cs]
        m_old = m_ref[mp, :, cs]
        m_new = jnp.maximum(m_old, jnp.max(st, axis=0, keepdims=True))
        alpha = jnp.exp2(m_old - m_new)
        p = jnp.exp2(st - m_new)
        l_ref[mp, :, cs] = l_ref[mp, :, cs] * alpha + jnp.sum(p, axis=0, keepdims=True)
        m_ref[mp, :, cs] = m_new
        p_ref[slot, mp, :, cs] = p.astype(BF16)
        alpha_ref[slot, mp, :, cs] = alpha

    def pv(vt, slot, mp, c):
        cs = slice(c * Q_STRIP, (c + 1) * Q_STRIP)
        acc_ref[mp, :, cs] = (acc_ref[mp, :, cs] * alpha_ref[slot, mp, :, cs]
                              + jnp.dot(vt, p_ref[slot, mp, :, cs], preferred_element_type=F32))

    def load_k(kidx):
        return k_ref[pl.ds(pl.multiple_of(kidx * tk, tk), tk), :]

    def key_of(i):
        return jnp.where(i < nd, nd * qi + i, i - nd)

    def iteration(s_slot, k_new, q_slot, vt, p_slot):
        for (mp, c) in units:
            softmax(s_slot, mp, c)
            scores(k_new, q_slot, None, mp, c)
            pv(vt, p_slot, mp, c)

    def softmax_all(slot):
        for (mp, c) in units:
            softmax(slot, mp, c)

    def pv_all(kidx, slot):
        vt = vt_ref[kidx]
        for (mp, c) in units:
            pv(vt, slot, mp, c)

    def masked_scores(slot, d):
        kd = load_k(nd * qi + d)
        for (mp, c) in units:
            scores(kd, slot, d, mp, c)

    masked_scores(0, 0)
    softmax_all(0)
    masked_scores(1, 1)

    def pair(jj, carry):
        j = 2 * jj + 2
        iteration(1, load_k(j - nd), 0, vt_ref[key_of(j - 2)], 0)
        iteration(0, load_k(j + 1 - nd), 1, vt_ref[key_of(j - 1)], 1)
        return carry

    lax.fori_loop(0, qi, pair, 0)
    n = nd * qi + nd
    softmax_all(1)
    pv_all(key_of(n - 2), 0)
    pv_all(key_of(n - 1), 1)

    lam = _lambda_value(lq_ref, lk_ref, lam_init)
    ot = acc_ref[0] * (1.0 / l_ref[0]) - lam * (acc_ref[1] * (1.0 / l_ref[1]))
    ms = jnp.mean(ot * ot, axis=0, keepdims=True)
    ot = ot * lax.rsqrt(ms + RMS_EPS) * (g_ref[...] * (1.0 - lam_init))
    o_ref[...] = ot.T.astype(o_ref.dtype)


def _flash_prompt(q, k, vt4, lam_q, lam_k, subln_g, *, batch, seq, tq, lam_init):
    tk = vt4.shape[-1]
    nkb = seq // tk
    q3 = q.reshape(batch, seq, D_MODEL)
    k3 = k.reshape(batch, seq, D_MODEL)
    kern = functools.partial(_flash_prompt_kernel, tq=tq, tk=tk, lam_init=lam_init)
    small = lambda shape: pl.BlockSpec(shape, lambda b, h, i: (0, 0))
    out = pl.pallas_call(
        kern,
        grid=(batch, N_HEADS, seq // tq),
        in_specs=[small((2, HEAD_DIM)), small((2, HEAD_DIM)), small((V_DIM, 1)),
                  pl.BlockSpec((None, tq, V_DIM), lambda b, h, i: (b, i, h)),
                  pl.BlockSpec((None, seq, V_DIM), lambda b, h, i: (b, 0, h)),
                  pl.BlockSpec((None, nkb, V_DIM, tk), lambda b, h, i: (h, b, 0, 0))],
        out_specs=pl.BlockSpec((None, tq, V_DIM), lambda b, h, i: (b, i, h)),
        out_shape=jax.ShapeDtypeStruct((batch, seq, D_MODEL), BF16),
        scratch_shapes=[pltpu.VMEM((2, 1, tq), F32), pltpu.VMEM((2, 1, tq), F32),
                        pltpu.VMEM((2, V_DIM, tq), F32), pltpu.VMEM((2, 2, tk, tq), F32),
                        pltpu.VMEM((2, 2, tk, tq), BF16), pltpu.VMEM((2, 2, 1, tq), F32)],
        compiler_params=_params(("parallel", "parallel", "arbitrary")),
        name="flash_prompt",
    )(lam_q, lam_k, subln_g.reshape(V_DIM, 1), q3, k3, vt4)
    return out.reshape(batch * seq, D_MODEL)


def _attn_sample_kernel(lq_ref, lk_ref, g_ref, q_ref, kn_ref, vn_ref, kc_ref, vc_ref, o_ref, *, lam_init):
    q = q_ref[...]
    kc = kc_ref[...].astype(BF16)
    vc = vc_ref[...].astype(BF16)
    kn = kn_ref[...]
    vn = vn_ref[...]
    accs, ls = [], []
    for mp in range(2):
        sl = slice(mp * HEAD_DIM, (mp + 1) * HEAD_DIM)
        s_past = _nt_dot(q[:, sl], kc[:, sl])
        s_new = _nt_dot(q[:, sl], kn[:, sl])
        mx = jnp.maximum(jnp.max(s_past, axis=-1, keepdims=True), jnp.max(s_new, axis=-1, keepdims=True))
        p_past = jnp.exp2(s_past - mx)
        p_new = jnp.exp2(s_new - mx)
        ls.append(jnp.sum(p_past, axis=-1, keepdims=True) + jnp.sum(p_new, axis=-1, keepdims=True))
        accs.append(jnp.dot(p_past.astype(BF16), vc, preferred_element_type=F32)
                    + jnp.dot(p_new.astype(BF16), vn, preferred_element_type=F32))
    lam = _lambda_value(lq_ref, lk_ref, lam_init)
    _subln_store(o_ref, accs[0], ls[0], accs[1], ls[1], lam, g_ref, lam_init)


def _attn_sample(q, k_new, v_new, cache_k, cache_v, lam_q, lam_k, subln_g, *, n_streams, dec_seq, past_len,
                 lam_init):
    kern = functools.partial(_attn_sample_kernel, lam_init=lam_init)
    small = lambda shape: pl.BlockSpec(shape, lambda b, h: (0, 0))
    new = pl.BlockSpec((dec_seq, V_DIM), lambda b, h: (b, h))
    past = pl.BlockSpec((None, past_len, V_DIM), lambda b, h: (b, 0, h))
    return pl.pallas_call(
        kern,
        grid=(n_streams, N_HEADS),
        in_specs=[small((2, HEAD_DIM)), small((2, HEAD_DIM)), small((1, V_DIM)), new, new, new, past, past],
        out_specs=new,
        out_shape=jax.ShapeDtypeStruct((n_streams * dec_seq, D_MODEL), BF16),
        compiler_params=_params(("parallel", "parallel")),
        name="attn_sample",
    )(lam_q, lam_k, subln_g.reshape(1, V_DIM), q, k_new, v_new, cache_k, cache_v)


def _attn_branch_kernel(o_ref, w_ref, g_ref, out_ref):
    o = o_ref[...]
    for c in range(D_MODEL // V_DIM):
        sl = slice(c * V_DIM, (c + 1) * V_DIM)
        acc = jnp.dot(o, w_ref[:, sl], preferred_element_type=F32)
        out_ref[:, sl] = (g_ref[:, sl].astype(F32) * acc).astype(out_ref.dtype)


def _conv_branch_kernel(dw_ref, lng_ref, lnb_ref, w_ref, g_ref, ga_ref, out_ref, act_ref):
    x = dw_ref[...].astype(F32)
    mu = jnp.mean(x, axis=-1, keepdims=True)
    xc = x - mu
    var = jnp.mean(xc * xc, axis=-1, keepdims=True)
    y = xc * lax.rsqrt(var + LN_EPS) * lng_ref[...] + lnb_ref[...]
    act_ref[...] = (y * _sigmoid(y)).astype(BF16)
    for c in range(D_MODEL // V_DIM):
        sl = slice(c * V_DIM, (c + 1) * V_DIM)
        acc = jnp.dot(act_ref[...], w_ref[:, sl], preferred_element_type=F32)
        merged = ga_ref[:, sl].astype(F32) + g_ref[:, sl].astype(F32) * acc
        out_ref[:, sl] = merged.astype(out_ref.dtype)


def _router_combine(logits):
    lane = lax.broadcasted_iota(jnp.int32, logits.shape, 1)
    big = jnp.int32(ROUTER_LANES)
    gmask = lane < N_GROUPS
    gl = jnp.where(gmask, logits, -jnp.inf)
    gmax = jnp.max(gl, axis=-1, keepdims=True)
    grp = jnp.min(jnp.where(gl == gmax, lane, big), axis=-1, keepdims=True)
    denom = jnp.sum(jnp.where(gmask, jnp.exp(gl - gmax), 0.0), axis=-1, keepdims=True)
    p_grp = 1.0 / denom
    eidx = lane - N_GROUPS
    emask = (eidx >= 0) & (eidx < N_EXPERTS) & ((eidx // EXPERTS_PER_GROUP) == grp)
    ev = jnp.where(emask, logits, -jnp.inf)
    l1 = jnp.max(ev, axis=-1, keepdims=True)
    i1 = jnp.min(jnp.where(ev == l1, lane, big), axis=-1, keepdims=True)
    ev2 = jnp.where(lane == i1, -jnp.inf, ev)
    l2 = jnp.max(ev2, axis=-1, keepdims=True)
    i2 = jnp.min(jnp.where(ev2 == l2, lane, big), axis=-1, keepdims=True)
    e2 = jnp.exp(l2 - l1)
    w1 = p_grp / (1.0 + e2)
    w2 = p_grp * e2 / (1.0 + e2)
    return jnp.where(lane == i1, w1, 0.0) + jnp.where(lane == i2, w2, 0.0)


def _out_proj_kernel(mg_ref, w_ref, x_ref, gffn_ref, wr_ref, br_ref, h_ref, hn_ref, comb_ref):
    mg = mg_ref[...]
    for c in range(D_MODEL // V_DIM):
        sl = slice(c * V_DIM, (c + 1) * V_DIM)
        h_ref[:, sl] = x_ref[:, sl] + jnp.dot(mg, w_ref[:, sl], preferred_element_type=F32)
    h = h_ref[...]
    ms = jnp.mean(h * h, axis=-1, keepdims=True)
    hn = h * lax.rsqrt(ms + RMS_EPS) * gffn_ref[...]
    hn_hi = hn.astype(BF16)
    hn_ref[...] = hn_hi
    hn_lo = (hn - hn_hi.astype(F32)).astype(BF16)
    hw = jnp.dot(hn_hi, wr_ref[...], preferred_element_type=F32)
    lw = jnp.dot(hn_lo, wr_ref[:, :ROUTER_LANES], preferred_element_type=F32)
    logits = hw[:, :ROUTER_LANES] + hw[:, ROUTER_LANES:] + lw + br_ref[...]
    comb_ref[...] = _router_combine(logits)


def _token_spec(tm, width=D_MODEL, col=0):
    return pl.BlockSpec((tm, width), lambda i: (i, col))


def _full_spec(shape):
    return pl.BlockSpec(shape, lambda i: (0,) * len(shape))


def _attn_branch(o, w_attn_bf, gates, *, tm):
    m = o.shape[0]
    return pl.pallas_call(
        _attn_branch_kernel,
        grid=(m // tm,),
        in_specs=[_token_spec(tm), _full_spec((ATTN_W, D_MODEL)), _token_spec(tm, D_MODEL, 0)],
        out_specs=_token_spec(tm),
        out_shape=jax.ShapeDtypeStruct((m, D_MODEL), BF16),
        compiler_params=_params(("parallel",)),
        name="attn_branch",
    )(o, w_attn_bf, gates)


def _conv_branch(dw, ln_g, ln_b, w_conv_bf, gates, ga, *, tm):
    m = dw.shape[0]
    return pl.pallas_call(
        _conv_branch_kernel,
        grid=(m // tm,),
        in_specs=[_token_spec(tm), _full_spec((1, C_CONV)), _full_spec((1, C_CONV)),
                  _full_spec((C_CONV, D_MODEL)), _token_spec(tm, D_MODEL, 1), _token_spec(tm)],
        out_specs=_token_spec(tm),
        out_shape=jax.ShapeDtypeStruct((m, D_MODEL), BF16),
        scratch_shapes=[pltpu.VMEM((tm, C_CONV), BF16)],
        compiler_params=_params(("parallel",)),
        name="conv_branch",
    )(dw, ln_g.reshape(1, C_CONV), ln_b.reshape(1, C_CONV), w_conv_bf, gates, ga)


def _out_proj(merged, w_out_bf, x, g_ffn, w_router, b_router, *, tm):
    m = merged.shape[0]
    return pl.pallas_call(
        _out_proj_kernel,
        grid=(m // tm,),
        in_specs=[_token_spec(tm), _full_spec((D_MODEL, D_MODEL)), _token_spec(tm), _full_spec((1, D_MODEL)),
                  _full_spec((D_MODEL, 2 * ROUTER_LANES)), _full_spec((1, ROUTER_LANES))],
        out_specs=[_token_spec(tm), _token_spec(tm), _token_spec(tm, ROUTER_LANES)],
        out_shape=[jax.ShapeDtypeStruct((m, D_MODEL), F32), jax.ShapeDtypeStruct((m, D_MODEL), BF16),
                   jax.ShapeDtypeStruct((m, ROUTER_LANES), F32)],
        compiler_params=_params(("parallel",)),
        name="out_proj_router",
    )(merged, w_out_bf, x, g_ffn.reshape(1, D_MODEL), w_router, b_router)


def _moe_dense_kernel(hn_ref, h_ref, comb_ref, wg_ref, wu_ref, wd_ref, gfin_ref, y_ref, acc_ref):
    e = pl.program_id(1)

    @pl.when(e == 0)
    def _():
        acc_ref[...] = h_ref[...]

    x = hn_ref[...]
    gate = jnp.dot(x, wg_ref[...], preferred_element_type=F32)
    up = jnp.dot(x, wu_ref[...], preferred_element_type=F32)
    comb = comb_ref[...]
    lane = lax.broadcasted_iota(jnp.int32, comb.shape, 1)
    c = jnp.sum(jnp.where(lane == e + N_GROUPS, comb, 0.0), axis=-1, keepdims=True)
    hid = (gate * _sigmoid(gate) * up * c).astype(BF16)
    acc_ref[...] += jnp.dot(hid, wd_ref[...], preferred_element_type=F32)

    @pl.when(e == N_EXPERTS - 1)
    def _():
        y = acc_ref[...]
        ms = jnp.mean(y * y, axis=-1, keepdims=True)
        y_ref[...] = y * lax.rsqrt(ms + RMS_EPS) * gfin_ref[...]


def _moe_dense(hn, h, comb, wg_bf, wu_bf, wd_bf, g_final, *, tm):
    m = hn.shape[0]
    return pl.pallas_call(
        _moe_dense_kernel,
        grid=(m // tm, N_EXPERTS),
        in_specs=[pl.BlockSpec((tm, D_MODEL), lambda i, e: (i, 0)),
                  pl.BlockSpec((tm, D_MODEL), lambda i, e: (i, 0)),
                  pl.BlockSpec((tm, ROUTER_LANES), lambda i, e: (i, 0)),
                  pl.BlockSpec((None, D_MODEL, D_EXPERT), lambda i, e: (e, 0, 0)),
                  pl.BlockSpec((None, D_MODEL, D_EXPERT), lambda i, e: (e, 0, 0)),
                  pl.BlockSpec((None, D_EXPERT, D_MODEL), lambda i, e: (e, 0, 0)),
                  pl.BlockSpec((1, D_MODEL), lambda i, e: (0, 0))],
        out_specs=pl.BlockSpec((tm, D_MODEL), lambda i, e: (i, 0)),
        out_shape=jax.ShapeDtypeStruct((m, D_MODEL), F32),
        scratch_shapes=[pltpu.VMEM((tm, D_MODEL), F32)],
        compiler_params=_params(("parallel", "arbitrary")),
        name="moe_dense",
    )(hn, h, comb, wg_bf, wu_bf, wd_bf, g_final.reshape(1, D_MODEL))


ATTN_W = N_HEADS * V_DIM


def _rope_tables(pos):
    half = HEAD_DIM // 2
    inv_freq = ROPE_THETA ** (-jnp.arange(half, dtype=F32) / half)
    ang = pos.astype(F32)[:, None] * inv_freq[None, :]
    cos, sin = jnp.cos(ang), jnp.sin(ang)
    return jnp.concatenate([cos, cos], axis=-1), jnp.concatenate([-sin, sin], axis=-1)


def _layer_tokens(x, pos_tables, init_state, attn_fn, wts, *, tm, n_seq, seq_len, tiles_per_seq, glu_tn, v_kind):
    cos_t, sin_t = pos_tables
    xn = _rmsnorm(x, wts["norm_mix_g"], tm)
    (q,) = _proj_rope(xn, wts["w_in"], 0, cos_t, sin_t, tm=tm, rope=True, scale=SCALE * LOG2_E,
                      outs=(("tokens", BF16),), name="proj_q")
    k_f32, k_bf = _proj_rope(xn, wts["w_in"], 1, cos_t, sin_t, tm=tm, rope=True, scale=1.0,
                             outs=(("tokens", F32), ("tokens", BF16)), name="proj_k")
    v_f32, v_bf = _proj_rope(xn, wts["w_in"], 2, cos_t, sin_t, tm=tm, rope=False, scale=1.0,
                             outs=(("tokens", F32), (v_kind, BF16)), name="proj_v")
    dw, new_conv = _proj_glu_conv(xn, wts["w_in"], 3, init_state, wts["w_dw"], wts["b_dw"],
                                  n_seq=n_seq, seq_len=seq_len, tiles_per_seq=tiles_per_seq, tn=glu_tn)
    gates = _proj_gate(xn, wts["w_in"], 5, wts["b_gate"], tm=tm)
    o = attn_fn(q, k_bf, v_bf)
    ga = _attn_branch(o, wts["w_attn"], gates, tm=tm)
    merged = _conv_branch(dw, wts["conv_ln_g"], wts["conv_ln_b"], wts["w_conv"], gates, ga, tm=tm)
    h, hn, comb = _out_proj(merged, wts["w_out"], x, wts["norm_ffn_g"], wts["w_router"], wts["b_router"], tm=tm)
    y = _moe_dense(hn, h, comb, wts["w_eg"], wts["w_eu"], wts["w_ed"], wts["norm_final_g"], tm=tm)
    return y, k_f32, v_f32, new_conv


def kernel(x_prompt, x_sample, cache_k, cache_v, state_conv, norm_mix_g, w_in, b_gate, lambda_q, lambda_k, subln_g, w_attn_branch, w_dw, b_dw, conv_ln_g, conv_ln_b, w_conv_branch, w_out, norm_ffn_g, w_router_group, b_router_group, w_router_expert, b_router_expert, w_exp_gate, w_exp_up, w_exp_down, norm_final_g):
    batch, seq, _ = x_prompt.shape
    n_streams, dec_seq, _ = x_sample.shape
    depth, _, past_len = cache_k.shape[:3]
    assert depth == 1
    assert past_len % CHUNK == 0 and dec_seq <= CHUNK
    l = 0
    lam_init = 0.8 - 0.6 * math.exp(-0.3 * l)

    pad_r = ROUTER_LANES - N_GROUPS - N_EXPERTS
    w_router = jnp.concatenate([w_router_group[l], w_router_expert[l], jnp.zeros((D_MODEL, pad_r), F32)], axis=1)
    b_router = jnp.concatenate([b_router_group[l], b_router_expert[l], jnp.zeros((pad_r,), F32)]).reshape(1, -1)
    w_router_hi = w_router.astype(BF16)
    w_router_lo = (w_router - w_router_hi.astype(F32)).astype(BF16)
    w_router = jnp.concatenate([w_router_hi, w_router_lo], axis=1)
    wts = dict(
        norm_mix_g=norm_mix_g[l], w_in=w_in[l].astype(BF16), b_gate=b_gate[l],
        w_attn=w_attn_branch[l].astype(BF16), w_dw=w_dw[l], b_dw=b_dw[l],
        conv_ln_g=conv_ln_g[l], conv_ln_b=conv_ln_b[l], w_conv=w_conv_branch[l].astype(BF16),
        w_out=w_out[l].astype(BF16), norm_ffn_g=norm_ffn_g[l], w_router=w_router, b_router=b_router,
        w_eg=w_exp_gate[l].astype(BF16), w_eu=w_exp_up[l].astype(BF16), w_ed=w_exp_down[l].astype(BF16),
        norm_final_g=norm_final_g)
    lam_q, lam_k, sub_g = lambda_q[l], lambda_k[l], subln_g[l]

    tm = 512
    tabs_p = _rope_tables(jnp.arange(seq, dtype=jnp.int32))
    init_p = jnp.zeros((batch, HIST_ROWS, C_CONV), F32)
    attn_p = functools.partial(_flash_prompt, lam_q=lam_q, lam_k=lam_k, subln_g=sub_g,
                               batch=batch, seq=seq, tq=2 * tm, lam_init=lam_init)
    y_p, k_p, v_p, conv_p = _layer_tokens(
        x_prompt.reshape(batch * seq, D_MODEL), tabs_p, init_p, attn_p, wts,
        tm=tm, n_seq=1, seq_len=tm, tiles_per_seq=seq // tm, glu_tn=512, v_kind="head_transposed")

    cos_s, sin_s = _rope_tables(past_len + jnp.arange(dec_seq, dtype=jnp.int32))
    tabs_s = (jnp.tile(cos_s, (n_streams, 1)), jnp.tile(sin_s, (n_streams, 1)))
    init_s = jnp.pad(state_conv[l], ((0, 0), (HIST_OFF, 0), (0, 0)))
    ck = cache_k[l].reshape(n_streams, past_len, D_MODEL)
    cv = cache_v[l].reshape(n_streams, past_len, D_MODEL)
    attn_s = functools.partial(_attn_sample, cache_k=ck, cache_v=cv, lam_q=lam_q, lam_k=lam_k, subln_g=sub_g,
                               n_streams=n_streams, dec_seq=dec_seq, past_len=past_len, lam_init=lam_init)
    m_s = n_streams * dec_seq
    y_s, k_s, v_s, conv_s = _layer_tokens(
        x_sample.reshape(m_s, D_MODEL), tabs_s, init_s, attn_s, wts,
        tm=m_s, n_seq=n_streams, seq_len=dec_seq, tiles_per_seq=1, glu_tn=512, v_kind="tokens")

    return (y_p.reshape(batch, seq, D_MODEL),
            y_s.reshape(n_streams, dec_seq, D_MODEL),
            k_p.reshape(1, batch, seq, N_HEADS, 2, HEAD_DIM),
            v_p.reshape(1, batch, seq, N_HEADS, V_DIM),
            conv_p.reshape(1, batch, CONV_K - 1, C_CONV),
            k_s.reshape(1, n_streams, dec_seq, N_HEADS, 2, HEAD_DIM),
            v_s.reshape(1, n_streams, dec_seq, N_HEADS, V_DIM),
            conv_s.reshape(1, n_streams, CONV_K - 1, C_CONV))
```

```python
import functools
import math

import jax
import jax.numpy as jnp
from jax import lax
from jax.experimental import pallas as pl
from jax.experimental.pallas import tpu as pltpu

F32 = jnp.float32
BF16 = jnp.bfloat16

D_MODEL = 2048
CHUNK = 64
HEAD_DIM = 128
N_HEADS = D_MODEL // (2 * HEAD_DIM)
V_DIM = 2 * HEAD_DIM
C_CONV = D_MODEL
CONV_K = 31
N_GROUPS = 4
EXPERTS_PER_GROUP = 4
N_EXPERTS = N_GROUPS * EXPERTS_PER_GROUP
D_EXPERT = D_MODEL // 4
ROPE_THETA = 10000.0
RMS_EPS = 1e-6
LN_EPS = 1e-5
NEG_INF = -1e30
SCALE = HEAD_DIM ** -0.5
LOG2_E = math.log2(math.e)
Q_STRIP = 256

LANES = 128
SUBLANES = 8
GLU_CHUNKS = 4
HIST_ROWS = 32
HIST_OFF = HIST_ROWS - (CONV_K - 1)
ROUTER_LANES = 128
VMEM_LIMIT = 56 * 1024 * 1024


def _params(sem):
    return pltpu.CompilerParams(dimension_semantics=sem, vmem_limit_bytes=VMEM_LIMIT)


def _sigmoid(x):
    return 1.0 / (1.0 + jnp.exp(-x))


def _rmsnorm_kernel(x_ref, g_ref, o_ref):
    x = x_ref[...]
    ms = jnp.mean(x * x, axis=-1, keepdims=True)
    o_ref[...] = (x * lax.rsqrt(ms + RMS_EPS) * g_ref[...]).astype(o_ref.dtype)


def _rmsnorm(x, g, tm):
    m = x.shape[0]
    return pl.pallas_call(
        _rmsnorm_kernel,
        grid=(m // tm,),
        in_specs=[pl.BlockSpec((tm, D_MODEL), lambda i: (i, 0)),
                  pl.BlockSpec((1, D_MODEL), lambda i: (0, 0))],
        out_specs=pl.BlockSpec((tm, D_MODEL), lambda i: (i, 0)),
        out_shape=jax.ShapeDtypeStruct((m, D_MODEL), BF16),
        compiler_params=_params(("parallel",)),
        name="rmsnorm_in",
    )(x, g.reshape(1, D_MODEL))


def _proj_rope_kernel(x_ref, w_ref, cos_ref, sin_ref, *out_refs, scale, rope, tn, out_kinds):
    x = x_ref[...]
    for c in range(tn // V_DIM):
        acc = jnp.dot(x, w_ref[:, c * V_DIM:(c + 1) * V_DIM], preferred_element_type=F32)
        if rope:
            cs = cos_ref[...]
            sn = sin_ref[...]
            halves = []
            for m in range(2):
                a = acc[:, m * HEAD_DIM:(m + 1) * HEAD_DIM]
                halves.append(a * cs + pltpu.roll(a, HEAD_DIM // 2, axis=1) * sn)
            acc = jnp.concatenate(halves, axis=1)
        if scale != 1.0:
            acc = acc * scale
        for o_ref, kind in zip(out_refs, out_kinds):
            if kind == "head_transposed":
                o_ref[c] = acc.T.astype(o_ref.dtype)
            else:
                o_ref[:, c * V_DIM:(c + 1) * V_DIM] = acc.astype(o_ref.dtype)


def _proj_rope(xn, w_bf, col_block, cos_t, sin_t, *, tm, rope, scale, outs, name):
    m = xn.shape[0]
    tn = D_MODEL
    t_tiles = cos_t.shape[0] // tm
    kinds = tuple(k for k, _ in outs)
    kern = functools.partial(_proj_rope_kernel, scale=scale, rope=rope, tn=tn, out_kinds=kinds)
    out_specs, out_shape = [], []
    for kind, dt in outs:
        if kind == "head_transposed":
            out_specs.append(pl.BlockSpec((N_HEADS, None, V_DIM, tm), lambda i: (0, i, 0, 0)))
            out_shape.append(jax.ShapeDtypeStruct((N_HEADS, m // tm, V_DIM, tm), dt))
        else:
            out_specs.append(pl.BlockSpec((tm, tn), lambda i: (i, 0)))
            out_shape.append(jax.ShapeDtypeStruct((m, tn), dt))
    return pl.pallas_call(
        kern,
        grid=(m // tm,),
        in_specs=[pl.BlockSpec((tm, D_MODEL), lambda i: (i, 0)),
                  pl.BlockSpec((D_MODEL, tn), lambda i: (0, col_block)),
                  pl.BlockSpec((tm, HEAD_DIM), lambda i: (i % t_tiles, 0)),
                  pl.BlockSpec((tm, HEAD_DIM), lambda i: (i % t_tiles, 0))],
        out_specs=out_specs,
        out_shape=out_shape,
        compiler_params=_params(("parallel",)),
        name=name,
    )(xn, w_bf, cos_t, sin_t)


def _proj_gate_kernel(x_ref, w_ref, b_ref, o_ref, *, tn):
    x = x_ref[...]
    for c in range(tn // V_DIM):
        sl = slice(c * V_DIM, (c + 1) * V_DIM)
        acc = jnp.dot(x, w_ref[:, sl], preferred_element_type=F32) + b_ref[:, sl]
        o_ref[:, sl] = _sigmoid(acc).astype(o_ref.dtype)


def _proj_gate(xn, w_bf, col_block, b_gate, *, tm):
    m = xn.shape[0]
    tn = D_MODEL
    kern = functools.partial(_proj_gate_kernel, tn=tn)
    return pl.pallas_call(
        kern,
        grid=(2, m // tm),
        in_specs=[pl.BlockSpec((tm, D_MODEL), lambda j, i: (i, 0)),
                  pl.BlockSpec((D_MODEL, tn), lambda j, i: (0, col_block + j)),
                  pl.BlockSpec((1, tn), lambda j, i: (0, j))],
        out_specs=pl.BlockSpec((tm, tn), lambda j, i: (i, j)),
        out_shape=jax.ShapeDtypeStruct((m, 2 * D_MODEL), BF16),
        compiler_params=_params(("parallel", "parallel")),
        name="proj_gate",
    )(xn, w_bf, b_gate.reshape(1, 2 * D_MODEL))


def _odd_stride(seq_len):
    s = -(-seq_len // SUBLANES)
    return s if s % 2 == 1 else s + 1


def _proj_glu_conv_kernel(x_ref, w_ref, init_ref, wdw_ref, bdw_ref, dw_ref, state_ref,
                          pad_ref, out_scr, carry_ref, *, n_seq, seq_len, tiles_per_seq):
    tile = pl.program_id(1)
    stride = _odd_stride(seq_len)
    pad_rows = pad_ref.shape[2]
    group = next(g for g in (13, 9, 5, 3, 1) if stride % g == 0)
    x = x_ref[...]
    for q in range(GLU_CHUNKS):
        ls = slice(q * LANES, (q + 1) * LANES)
        ab = jnp.dot(x, w_ref[:, q * 2 * LANES:(q + 1) * 2 * LANES], preferred_element_type=F32)
        glu = ab[:, :LANES] * _sigmoid(ab[:, LANES:])
        for s in range(n_seq):
            if tiles_per_seq == 1:
                pad_ref[q, s, 0:HIST_ROWS, :] = init_ref[s, :, ls]
            else:
                first = (tile % tiles_per_seq) == 0
                pad_ref[q, s, 0:HIST_ROWS, :] = jnp.where(first, init_ref[s, :, ls], carry_ref[q])
            pad_ref[q, s, HIST_ROWS:HIST_ROWS + seq_len, :] = glu[s * seq_len:(s + 1) * seq_len, :]
            pad_ref[q, s, HIST_ROWS + seq_len:pad_rows, :] = jnp.zeros((pad_rows - HIST_ROWS - seq_len, LANES), F32)
            if tiles_per_seq != 1:
                carry_ref[q] = pad_ref[q, s, seq_len:seq_len + HIST_ROWS, :]
            state_ref[s, :, ls] = pad_ref[q, s, seq_len + HIST_OFF:seq_len + HIST_ROWS, :]

        bias = jnp.broadcast_to(bdw_ref[:, ls], (SUBLANES, LANES))
        for s in range(n_seq):
            for g0 in range(0, stride, group):
                accs = [bias] * group
                for j in range(CONV_K):
                    w = jnp.broadcast_to(wdw_ref[j:j + 1, ls], (SUBLANES, LANES))
                    for k in range(group):
                        rows = pl.ds(g0 + k + HIST_OFF + j, SUBLANES, stride=stride)
                        accs[k] = accs[k] + w * pad_ref[q, s, rows, :]
                for k in range(group):
                    out_scr[q, s, pl.ds(g0 + k, SUBLANES, stride=stride), :] = accs[k]
            dw_ref[s * seq_len:(s + 1) * seq_len, ls] = out_scr[q, s, 0:seq_len, :].astype(dw_ref.dtype)


def _proj_glu_conv(xn, w_glu, init_state, w_dw, b_dw, *, n_seq, seq_len, tiles_per_seq):
    m = xn.shape[0]
    tm = n_seq * seq_len
    uw = GLU_CHUNKS * LANES
    n_sequences = init_state.shape[0]
    stride = _odd_stride(seq_len)
    pad_rows = -(-(HIST_ROWS + SUBLANES * stride + SUBLANES) // SUBLANES) * SUBLANES
    kern = functools.partial(_proj_glu_conv_kernel, n_seq=n_seq, seq_len=seq_len, tiles_per_seq=tiles_per_seq)
    return pl.pallas_call(
        kern,
        grid=(C_CONV // uw, m // tm),
        in_specs=[pl.BlockSpec((tm, D_MODEL), lambda c, i: (i, 0)),
                  pl.BlockSpec((None, D_MODEL, 2 * uw), lambda c, i: (c, 0, 0)),
                  pl.BlockSpec((n_seq, HIST_ROWS, uw), lambda c, i: (i // tiles_per_seq, 0, c)),
                  pl.BlockSpec((CONV_K, uw), lambda c, i: (0, c)),
                  pl.BlockSpec((1, uw), lambda c, i: (0, c))],
        out_specs=[pl.BlockSpec((tm, uw), lambda c, i: (i, c)),
                   pl.BlockSpec((n_seq, CONV_K - 1, uw), lambda c, i: (i // tiles_per_seq, 0, c))],
        out_shape=[jax.ShapeDtypeStruct((m, C_CONV), BF16),
                   jax.ShapeDtypeStruct((n_sequences, CONV_K - 1, C_CONV), F32)],
        scratch_shapes=[pltpu.VMEM((GLU_CHUNKS, n_seq, pad_rows, LANES), F32),
                        pltpu.VMEM((GLU_CHUNKS, n_seq, SUBLANES * stride, LANES), F32),
                        pltpu.VMEM((GLU_CHUNKS, HIST_ROWS, LANES), F32)],
        compiler_params=_params(("parallel", "arbitrary")),
        name="proj_glu_conv",
    )(xn, w_glu, init_state, w_dw, b_dw.reshape(1, C_CONV))


def _lambda_value(lq_ref, lk_ref, lam_init):
    prod = lq_ref[...] * lk_ref[...]
    d = jnp.sum(prod, axis=1, keepdims=True)
    e = jnp.exp(d)
    return e[0:1, :] - e[1:2, :] + lam_init


def _subln_store(o_ref, acc0, l0, acc1, l1, lam, g_ref, lam_init):
    o = acc0 / l0 - lam * (acc1 / l1)
    ms = jnp.mean(o * o, axis=-1, keepdims=True)
    o = o * lax.rsqrt(ms + RMS_EPS) * g_ref[...] * (1.0 - lam_init)
    o_ref[...] = o.astype(o_ref.dtype)


def _nt_dot(a, b):
    return lax.dot_general(a, b, (((1,), (1,)), ((), ())), preferred_element_type=F32)


def _flash_prompt_kernel(lq_ref, lk_ref, g_ref, q_ref, k_ref, vt_ref, o_ref,
                         m_ref, l_ref, acc_ref, st_ref, p_ref, alpha_ref, *, tq, tk, lam_init):
    nd = tq // tk
    assert nd == 2
    qi = pl.program_id(2)
    q = q_ref[...]
    m_ref[...] = jnp.full(m_ref.shape, NEG_INF, F32)
    l_ref[...] = jnp.zeros(l_ref.shape, F32)
    acc_ref[...] = jnp.zeros(acc_ref.shape, F32)
    units = [(mp, c) for c in range(tq // Q_STRIP) for mp in range(2)]

    def scores(k, slot, diag, mp, c):
        sl = slice(mp * HEAD_DIM, (mp + 1) * HEAD_DIM)
        cs = slice(c * Q_STRIP, (c + 1) * Q_STRIP)
        st = _nt_dot(k[:, sl], q[cs, sl])
        if diag is not None:
            key_chunk = (lax.broadcasted_iota(jnp.int32, (tk, Q_STRIP), 0) + diag * tk) // CHUNK
            qry_chunk = (lax.broadcasted_iota(jnp.int32, (tk, Q_STRIP), 1) + c * Q_STRIP) // CHUNK
            st = jnp.where(key_chunk <= qry_chunk, st, NEG_INF)
        st_ref[slot, mp, :, cs] = st

    def softmax(slot, mp, c):
        cs = slice(c * Q_STRIP, (c + 1) * Q_STRIP)
        st = st_ref[slot, mp, :, cs]
        m_old = m_ref[mp, :, cs]
        m_new = jnp.maximum(m_old, jnp.max(st, axis=0, keepdims=True))
        alpha = jnp.exp2(m_old - m_new)
        p = jnp.exp2(st - m_new)
        l_ref[mp, :, cs] = l_ref[mp, :, cs] * alpha + jnp.sum(p, axis=0, keepdims=True)
        m_ref[mp, :, cs] = m_new
        p_ref[slot, mp, :, cs] = p.astype(BF16)
        alpha_ref[slot, mp, :, cs] = alpha

    def pv(vt, slot, mp, c):
        cs = slice(c * Q_STRIP, (c + 1) * Q_STRIP)
        acc_ref[mp, :, cs] = (acc_ref[mp, :, cs] * alpha_ref[slot, mp, :, cs]
                              + jnp.dot(vt, p_ref[slot, mp, :, cs], preferred_element_type=F32))

    def load_k(kidx):
        return k_ref[pl.ds(pl.multiple_of(kidx * tk, tk), tk), :]

    def key_of(i):
        return jnp.where(i < nd, nd * qi + i, i - nd)

    def iteration(s_slot, k_new, q_slot, vt, p_slot):
        for (mp, c) in units:
            softmax(s_slot, mp, c)
            scores(k_new, q_slot, None, mp, c)
            pv(vt, p_slot, mp, c)

    def softmax_all(slot):
        for (mp, c) in units:
            softmax(slot, mp, c)

    def pv_all(kidx, slot):
        vt = vt_ref[kidx]
        for (mp, c) in units:
            pv(vt, slot, mp, c)

    def masked_scores(slot, d):
        kd = load_k(nd * qi + d)
        for (mp, c) in units:
            scores(kd, slot, d, mp, c)

    masked_scores(0, 0)
    softmax_all(0)
    masked_scores(1, 1)

    def pair(jj, carry):
        j = 2 * jj + 2
        iteration(1, load_k(j - nd), 0, vt_ref[key_of(j - 2)], 0)
        iteration(0, load_k(j + 1 - nd), 1, vt_ref[key_of(j - 1)], 1)
        return carry

    lax.fori_loop(0, qi, pair, 0)
    n = nd * qi + nd
    softmax_all(1)
    pv_all(key_of(n - 2), 0)
    pv_all(key_of(n - 1), 1)

    lam = _lambda_value(lq_ref, lk_ref, lam_init)
    ot = acc_ref[0] * (1.0 / l_ref[0]) - lam * (acc_ref[1] * (1.0 / l_ref[1]))
    ms = jnp.mean(ot * ot, axis=0, keepdims=True)
    ot = ot * lax.rsqrt(ms + RMS_EPS) * (g_ref[...] * (1.0 - lam_init))
    o_ref[...] = ot.T.astype(o_ref.dtype)


def _flash_prompt(q, k, vt4, lam_q, lam_k, subln_g, *, batch, seq, tq, lam_init):
    tk = vt4.shape[-1]
    nkb = seq // tk
    q3 = q.reshape(batch, seq, D_MODEL)
    k3 = k.reshape(batch, seq, D_MODEL)
    kern = functools.partial(_flash_prompt_kernel, tq=tq, tk=tk, lam_init=lam_init)
    small = lambda shape: pl.BlockSpec(shape, lambda b, h, i: (0, 0))
    out = pl.pallas_call(
        kern,
        grid=(batch, N_HEADS, seq // tq),
        in_specs=[small((2, HEAD_DIM)), small((2, HEAD_DIM)), small((V_DIM, 1)),
                  pl.BlockSpec((None, tq, V_DIM), lambda b, h, i: (b, i, h)),
                  pl.BlockSpec((None, seq, V_DIM), lambda b, h, i: (b, 0, h)),
                  pl.BlockSpec((None, nkb, V_DIM, tk), lambda b, h, i: (h, b, 0, 0))],
        out_specs=pl.BlockSpec((None, tq, V_DIM), lambda b, h, i: (b, i, h)),
        out_shape=jax.ShapeDtypeStruct((batch, seq, D_MODEL), BF16),
        scratch_shapes=[pltpu.VMEM((2, 1, tq), F32), pltpu.VMEM((2, 1, tq), F32),
                        pltpu.VMEM((2, V_DIM, tq), F32), pltpu.VMEM((2, 2, tk, tq), F32),
                        pltpu.VMEM((2, 2, tk, tq), BF16), pltpu.VMEM((2, 2, 1, tq), F32)],
        compiler_params=_params(("parallel", "parallel", "arbitrary")),
        name="flash_prompt",
    )(lam_q, lam_k, subln_g.reshape(V_DIM, 1), q3, k3, vt4)
    return out.reshape(batch * seq, D_MODEL)


def _attn_sample_kernel(lq_ref, lk_ref, g_ref, q_ref, kn_ref, vn_ref, kc_ref, vc_ref, o_ref, *, lam_init):
    q = q_ref[...]
    kc = kc_ref[...].astype(BF16)
    vc = vc_ref[...].astype(BF16)
    kn = kn_ref[...]
    vn = vn_ref[...]
    accs, ls = [], []
    for mp in range(2):
        sl = slice(mp * HEAD_DIM, (mp + 1) * HEAD_DIM)
        s_past = _nt_dot(q[:, sl], kc[:, sl])
        s_new = _nt_dot(q[:, sl], kn[:, sl])
        mx = jnp.maximum(jnp.max(s_past, axis=-1, keepdims=True), jnp.max(s_new, axis=-1, keepdims=True))
        p_past = jnp.exp2(s_past - mx)
        p_new = jnp.exp2(s_new - mx)
        ls.append(jnp.sum(p_past, axis=-1, keepdims=True) + jnp.sum(p_new, axis=-1, keepdims=True))
        accs.append(jnp.dot(p_past.astype(BF16), vc, preferred_element_type=F32)
                    + jnp.dot(p_new.astype(BF16), vn, preferred_element_type=F32))
    lam = _lambda_value(lq_ref, lk_ref, lam_init)
    _subln_store(o_ref, accs[0], ls[0], accs[1], ls[1], lam, g_ref, lam_init)


def _attn_sample(q, k_new, v_new, cache_k, cache_v, lam_q, lam_k, subln_g, *, n_streams, dec_seq, past_len,
                 lam_init):
    kern = functools.partial(_attn_sample_kernel, lam_init=lam_init)
    small = lambda shape: pl.BlockSpec(shape, lambda b, h: (0, 0))
    new = pl.BlockSpec((dec_seq, V_DIM), lambda b, h: (b, h))
    past = pl.BlockSpec((None, past_len, V_DIM), lambda b, h: (b, 0, h))
    return pl.pallas_call(
        kern,
        grid=(n_streams, N_HEADS),
        in_specs=[small((2, HEAD_DIM)), small((2, HEAD_DIM)), small((1, V_DIM)), new, new, new, past, past],
        out_specs=new,
        out_shape=jax.ShapeDtypeStruct((n_streams * dec_seq, D_MODEL), BF16),
        compiler_params=_params(("parallel", "parallel")),
        name="attn_sample",
    )(lam_q, lam_k, subln_g.reshape(1, V_DIM), q, k_new, v_new, cache_k, cache_v)


def _attn_branch_kernel(o_ref, w_ref, g_ref, out_ref):
    o = o_ref[...]
    for c in range(D_MODEL // V_DIM):
        sl = slice(c * V_DIM, (c + 1) * V_DIM)
        acc = jnp.dot(o, w_ref[:, sl], preferred_element_type=F32)
        out_ref[:, sl] = (g_ref[:, sl].astype(F32) * acc).astype(out_ref.dtype)


def _conv_branch_kernel(dw_ref, lng_ref, lnb_ref, w_ref, g_ref, ga_ref, out_ref, act_ref):
    x = dw_ref[...].astype(F32)
    mu = jnp.mean(x, axis=-1, keepdims=True)
    xc = x - mu
    var = jnp.mean(xc * xc, axis=-1, keepdims=True)
    y = xc * lax.rsqrt(var + LN_EPS) * lng_ref[...] + lnb_ref[...]
    act_ref[...] = (y * _sigmoid(y)).astype(BF16)
    for c in range(D_MODEL // V_DIM):
        sl = slice(c * V_DIM, (c + 1) * V_DIM)
        acc = jnp.dot(act_ref[...], w_ref[:, sl], preferred_element_type=F32)
        merged = ga_ref[:, sl].astype(F32) + g_ref[:, sl].astype(F32) * acc
        out_ref[:, sl] = merged.astype(out_ref.dtype)


def _router_combine(logits):
    lane = lax.broadcasted_iota(jnp.int32, logits.shape, 1)
    big = jnp.int32(ROUTER_LANES)
    gmask = lane < N_GROUPS
    gl = jnp.where(gmask, logits, -jnp.inf)
    gmax = jnp.max(gl, axis=-1, keepdims=True)
    grp = jnp.min(jnp.where(gl == gmax, lane, big), axis=-1, keepdims=True)
    denom = jnp.sum(jnp.where(gmask, jnp.exp(gl - gmax), 0.0), axis=-1, keepdims=True)
    p_grp = 1.0 / denom
    eidx = lane - N_GROUPS
    emask = (eidx >= 0) & (eidx < N_EXPERTS) & ((eidx // EXPERTS_PER_GROUP) == grp)
    ev = jnp.where(emask, logits, -jnp.inf)
    l1 = jnp.max(ev, axis=-1, keepdims=True)
    i1 = jnp.min(jnp.where(ev == l1, lane, big), axis=-1, keepdims=True)
    ev2 = jnp.where(lane == i1, -jnp.inf, ev)
    l2 = jnp.max(ev2, axis=-1, keepdims=True)
    i2 = jnp.min(jnp.where(ev2 == l2, lane, big), axis=-1, keepdims=True)
    e2 = jnp.exp(l2 - l1)
    w1 = p_grp / (1.0 + e2)
    w2 = p_grp * e2 / (1.0 + e2)
    return jnp.where(lane == i1, w1, 0.0) + jnp.where(lane == i2, w2, 0.0)


def _out_proj_kernel(mg_ref, w_ref, x_ref, gffn_ref, wr_ref, br_ref, h_ref, hn_ref, comb_ref):
    mg = mg_ref[...]
    for c in range(D_MODEL // V_DIM):
        sl = slice(c * V_DIM, (c + 1) * V_DIM)
        h_ref[:, sl] = x_ref[:, sl] + jnp.dot(mg, w_ref[:, sl], preferred_element_type=F32)
    h = h_ref[...]
    ms = jnp.mean(h * h, axis=-1, keepdims=True)
    hn = h * lax.rsqrt(ms + RMS_EPS) * gffn_ref[...]
    hn_hi = hn.astype(BF16)
    hn_ref[...] = hn_hi
    hn_lo = (hn - hn_hi.astype(F32)).astype(BF16)
    hw = jnp.dot(hn_hi, wr_ref[...], preferred_element_type=F32)
    lw = jnp.dot(hn_lo, wr_ref[:, :ROUTER_LANES], preferred_element_type=F32)
    logits = hw[:, :ROUTER_LANES] + hw[:, ROUTER_LANES:] + lw + br_ref[...]
    comb_ref[...] = _router_combine(logits)


def _token_spec(tm, width=D_MODEL, col=0):
    return pl.BlockSpec((tm, width), lambda i: (i, col))


def _full_spec(shape):
    return pl.BlockSpec(shape, lambda i: (0,) * len(shape))


def _attn_branch(o, w_attn_bf, gates, *, tm):
    m = o.shape[0]
    return pl.pallas_call(
        _attn_branch_kernel,
        grid=(m // tm,),
        in_specs=[_token_spec(tm), _full_spec((ATTN_W, D_MODEL)), _token_spec(tm, D_MODEL, 0)],
        out_specs=_token_spec(tm),
        out_shape=jax.ShapeDtypeStruct((m, D_MODEL), BF16),
        compiler_params=_params(("parallel",)),
        name="attn_branch",
    )(o, w_attn_bf, gates)


def _conv_branch(dw, ln_g, ln_b, w_conv_bf, gates, ga, *, tm):
    m = dw.shape[0]
    return pl.pallas_call(
        _conv_branch_kernel,
        grid=(m // tm,),
        in_specs=[_token_spec(tm), _full_spec((1, C_CONV)), _full_spec((1, C_CONV)),
                  _full_spec((C_CONV, D_MODEL)), _token_spec(tm, D_MODEL, 1), _token_spec(tm)],
        out_specs=_token_spec(tm),
        out_shape=jax.ShapeDtypeStruct((m, D_MODEL), BF16),
        scratch_shapes=[pltpu.VMEM((tm, C_CONV), BF16)],
        compiler_params=_params(("parallel",)),
        name="conv_branch",
    )(dw, ln_g.reshape(1, C_CONV), ln_b.reshape(1, C_CONV), w_conv_bf, gates, ga)


def _out_proj(merged, w_out_bf, x, g_ffn, w_router, b_router, *, tm):
    m = merged.shape[0]
    return pl.pallas_call(
        _out_proj_kernel,
        grid=(m // tm,),
        in_specs=[_token_spec(tm), _full_spec((D_MODEL, D_MODEL)), _token_spec(tm), _full_spec((1, D_MODEL)),
                  _full_spec((D_MODEL, 2 * ROUTER_LANES)), _full_spec((1, ROUTER_LANES))],
        out_specs=[_token_spec(tm), _token_spec(tm), _token_spec(tm, ROUTER_LANES)],
        out_shape=[jax.ShapeDtypeStruct((m, D_MODEL), F32), jax.ShapeDtypeStruct((m, D_MODEL), BF16),
                   jax.ShapeDtypeStruct((m, ROUTER_LANES), F32)],
        compiler_params=_params(("parallel",)),
        name="out_proj_router",
    )(merged, w_out_bf, x, g_ffn.reshape(1, D_MODEL), w_router, b_router)


def _moe_dense_kernel(hn_ref, h_ref, comb_ref, wg_ref, wu_ref, wd_ref, gfin_ref, y_ref, acc_ref):
    e = pl.program_id(1)

    @pl.when(e == 0)
    def _():
        acc_ref[...] = h_ref[...]

    x = hn_ref[...]
    gate = jnp.dot(x, wg_ref[...], preferred_element_type=F32)
    up = jnp.dot(x, wu_ref[...], preferred_element_type=F32)
    comb = comb_ref[...]
    lane = lax.broadcasted_iota(jnp.int32, comb.shape, 1)
    c = jnp.sum(jnp.where(lane == e + N_GROUPS, comb, 0.0), axis=-1, keepdims=True)
    hid = (gate * _sigmoid(gate) * up * c).astype(BF16)
    acc_ref[...] += jnp.dot(hid, wd_ref[...], preferred_element_type=F32)

    @pl.when(e == N_EXPERTS - 1)
    def _():
        y = acc_ref[...]
        ms = jnp.mean(y * y, axis=-1, keepdims=True)
        y_ref[...] = y * lax.rsqrt(ms + RMS_EPS) * gfin_ref[...]


def _moe_dense(hn, h, comb, wg_bf, wu_bf, wd_bf, g_final, *, tm):
    m = hn.shape[0]
    return pl.pallas_call(
        _moe_dense_kernel,
        grid=(m // tm, N_EXPERTS),
        in_specs=[pl.BlockSpec((tm, D_MODEL), lambda i, e: (i, 0)),
                  pl.BlockSpec((tm, D_MODEL), lambda i, e: (i, 0)),
                  pl.BlockSpec((tm, ROUTER_LANES), lambda i, e: (i, 0)),
                  pl.BlockSpec((None, D_MODEL, D_EXPERT), lambda i, e: (e, 0, 0)),
                  pl.BlockSpec((None, D_MODEL, D_EXPERT), lambda i, e: (e, 0, 0)),
                  pl.BlockSpec((None, D_EXPERT, D_MODEL), lambda i, e: (e, 0, 0)),
                  pl.BlockSpec((1, D_MODEL), lambda i, e: (0, 0))],
        out_specs=pl.BlockSpec((tm, D_MODEL), lambda i, e: (i, 0)),
        out_shape=jax.ShapeDtypeStruct((m, D_MODEL), F32),
        scratch_shapes=[pltpu.VMEM((tm, D_MODEL), F32)],
        compiler_params=_params(("parallel", "arbitrary")),
        name="moe_dense",
    )(hn, h, comb, wg_bf, wu_bf, wd_bf, g_final.reshape(1, D_MODEL))


ATTN_W = N_HEADS * V_DIM


def _rope_tables(pos):
    half = HEAD_DIM // 2
    inv_freq = ROPE_THETA ** (-jnp.arange(half, dtype=F32) / half)
    ang = pos.astype(F32)[:, None] * inv_freq[None, :]
    cos, sin = jnp.cos(ang), jnp.sin(ang)
    return jnp.concatenate([cos, cos], axis=-1), jnp.concatenate([-sin, sin], axis=-1)


def _layer_tokens(x, pos_tables, init_state, attn_fn, wts, *, tm, n_seq, seq_len, tiles_per_seq, v_kind):
    cos_t, sin_t = pos_tables
    xn = _rmsnorm(x, wts["norm_mix_g"], tm)
    (q,) = _proj_rope(xn, wts["w_in"], 0, cos_t, sin_t, tm=tm, rope=True, scale=SCALE * LOG2_E,
                      outs=(("tokens", BF16),), name="proj_q")
    k_f32, k_bf = _proj_rope(xn, wts["w_in"], 1, cos_t, sin_t, tm=tm, rope=True, scale=1.0,
                             outs=(("tokens", F32), ("tokens", BF16)), name="proj_k")
    v_f32, v_bf = _proj_rope(xn, wts["w_in"], 2, cos_t, sin_t, tm=tm, rope=False, scale=1.0,
                             outs=(("tokens", F32), (v_kind, BF16)), name="proj_v")
    dw, new_conv = _proj_glu_conv(xn, wts["w_glu"], init_state, wts["w_dw"], wts["b_dw"],
                                  n_seq=n_seq, seq_len=seq_len, tiles_per_seq=tiles_per_seq)
    gates = _proj_gate(xn, wts["w_in"], 5, wts["b_gate"], tm=tm)
    o = attn_fn(q, k_bf, v_bf)
    ga = _attn_branch(o, wts["w_attn"], gates, tm=tm)
    merged = _conv_branch(dw, wts["conv_ln_g"], wts["conv_ln_b"], wts["w_conv"], gates, ga, tm=tm)
    h, hn, comb = _out_proj(merged, wts["w_out"], x, wts["norm_ffn_g"], wts["w_router"], wts["b_router"], tm=tm)
    y = _moe_dense(hn, h, comb, wts["w_eg"], wts["w_eu"], wts["w_ed"], wts["norm_final_g"], tm=tm)
    return y, k_f32, v_f32, new_conv


def kernel(x_prompt, x_sample, cache_k, cache_v, state_conv, norm_mix_g, w_in, b_gate, lambda_q, lambda_k, subln_g, w_attn_branch, w_dw, b_dw, conv_ln_g, conv_ln_b, w_conv_branch, w_out, norm_ffn_g, w_router_group, b_router_group, w_router_expert, b_router_expert, w_exp_gate, w_exp_up, w_exp_down, norm_final_g):
    batch, seq, _ = x_prompt.shape
    n_streams, dec_seq, _ = x_sample.shape
    depth, _, past_len = cache_k.shape[:3]
    assert depth == 1
    assert past_len % CHUNK == 0 and dec_seq <= CHUNK
    l = 0
    lam_init = 0.8 - 0.6 * math.exp(-0.3 * l)

    pad_r = ROUTER_LANES - N_GROUPS - N_EXPERTS
    w_router = jnp.concatenate([w_router_group[l], w_router_expert[l], jnp.zeros((D_MODEL, pad_r), F32)], axis=1)
    b_router = jnp.concatenate([b_router_group[l], b_router_expert[l], jnp.zeros((pad_r,), F32)]).reshape(1, -1)
    w_router_hi = w_router.astype(BF16)
    w_router_lo = (w_router - w_router_hi.astype(F32)).astype(BF16)
    w_router = jnp.concatenate([w_router_hi, w_router_lo], axis=1)
    w_in_bf = w_in[l].astype(BF16)
    n_chunks = C_CONV // LANES
    glu_cols = N_HEADS * (4 * HEAD_DIM + V_DIM)
    w_val = w_in_bf[:, glu_cols:glu_cols + C_CONV].reshape(D_MODEL, n_chunks, 1, LANES)
    w_gat = w_in_bf[:, glu_cols + C_CONV:glu_cols + 2 * C_CONV].reshape(D_MODEL, n_chunks, 1, LANES)
    w_glu = jnp.concatenate([w_val, w_gat], axis=2).reshape(D_MODEL, n_chunks // GLU_CHUNKS, GLU_CHUNKS * 2 * LANES)
    w_glu = w_glu.transpose(1, 0, 2)
    wts = dict(
        norm_mix_g=norm_mix_g[l], w_in=w_in_bf, w_glu=w_glu, b_gate=b_gate[l],
        w_attn=w_attn_branch[l].astype(BF16), w_dw=w_dw[l], b_dw=b_dw[l],
        conv_ln_g=conv_ln_g[l], conv_ln_b=conv_ln_b[l], w_conv=w_conv_branch[l].astype(BF16),
        w_out=w_out[l].astype(BF16), norm_ffn_g=norm_ffn_g[l], w_router=w_router, b_router=b_router,
        w_eg=w_exp_gate[l].astype(BF16), w_eu=w_exp_up[l].astype(BF16), w_ed=w_exp_down[l].astype(BF16),
        norm_final_g=norm_final_g)
    lam_q, lam_k, sub_g = lambda_q[l], lambda_k[l], subln_g[l]

    tm = 512
    tabs_p = _rope_tables(jnp.arange(seq, dtype=jnp.int32))
    init_p = jnp.zeros((batch, HIST_ROWS, C_CONV), F32)
    attn_p = functools.partial(_flash_prompt, lam_q=lam_q, lam_k=lam_k, subln_g=sub_g,
                               batch=batch, seq=seq, tq=2 * tm, lam_init=lam_init)
    y_p, k_p, v_p, conv_p = _layer_tokens(
        x_prompt.reshape(batch * seq, D_MODEL), tabs_p, init_p, attn_p, wts,
        tm=tm, n_seq=1, seq_len=tm, tiles_per_seq=seq // tm, v_kind="head_transposed")

    cos_s, sin_s = _rope_tables(past_len + jnp.arange(dec_seq, dtype=jnp.int32))
    tabs_s = (jnp.tile(cos_s, (n_streams, 1)), jnp.tile(sin_s, (n_streams, 1)))
    init_s = jnp.pad(state_conv[l], ((0, 0), (HIST_OFF, 0), (0, 0)))
    ck = cache_k[l].reshape(n_streams, past_len, D_MODEL)
    cv = cache_v[l].reshape(n_streams, past_len, D_MODEL)
    attn_s = functools.partial(_attn_sample, cache_k=ck, cache_v=cv, lam_q=lam_q, lam_k=lam_k, subln_g=sub_g,
                               n_streams=n_streams, dec_seq=dec_seq, past_len=past_len, lam_init=lam_init)
    m_s = n_streams * dec_seq
    y_s, k_s, v_s, conv_s = _layer_tokens(
        x_sample.reshape(m_s, D_MODEL), tabs_s, init_s, attn_s, wts,
        tm=m_s, n_seq=n_streams, seq_len=dec_seq, tiles_per_seq=1, v_kind="tokens")

    return (y_p.reshape(batch, seq, D_MODEL),
            y_s.reshape(n_streams, dec_seq, D_MODEL),
            k_p.reshape(1, batch, seq, N_HEADS, 2, HEAD_DIM),
            v_p.reshape(1, batch, seq, N_HEADS, V_DIM),
            conv_p.reshape(1, batch, CONV_K - 1, C_CONV),
            k_s.reshape(1, n_streams, dec_seq, N_HEADS, 2, HEAD_DIM),
            v_s.reshape(1, n_streams, dec_seq, N_HEADS, V_DIM),
            conv_s.reshape(1, n_streams, CONV_K - 1, C_CONV))
```

```python
import functools
import math

import jax
import jax.numpy as jnp
from jax import lax
from jax.experimental import pallas as pl
from jax.experimental.pallas import tpu as pltpu

F32 = jnp.float32
BF16 = jnp.bfloat16

D_MODEL = 2048
CHUNK = 64
HEAD_DIM = 128
N_HEADS = D_MODEL // (2 * HEAD_DIM)
V_DIM = 2 * HEAD_DIM
C_CONV = D_MODEL
CONV_K = 31
N_GROUPS = 4
EXPERTS_PER_GROUP = 4
N_EXPERTS = N_GROUPS * EXPERTS_PER_GROUP
D_EXPERT = D_MODEL // 4
ROPE_THETA = 10000.0
RMS_EPS = 1e-6
LN_EPS = 1e-5
NEG_INF = -1e30
SCALE = HEAD_DIM ** -0.5
LOG2_E = math.log2(math.e)
Q_STRIP = 256

LANES = 128
SUBLANES = 8
GLU_CHUNKS = 4
MOE_ROWS = 512
HIST_ROWS = 32
HIST_OFF = HIST_ROWS - (CONV_K - 1)
ROUTER_LANES = 128
VMEM_LIMIT = 56 * 1024 * 1024


def _params(sem):
    return pltpu.CompilerParams(dimension_semantics=sem, vmem_limit_bytes=VMEM_LIMIT)


def _sigmoid(x):
    return 1.0 / (1.0 + jnp.exp(-x))


def _rmsnorm_kernel(x_ref, g_ref, o_ref):
    x = x_ref[...]
    ms = jnp.mean(x * x, axis=-1, keepdims=True)
    o_ref[...] = (x * lax.rsqrt(ms + RMS_EPS) * g_ref[...]).astype(o_ref.dtype)


def _rmsnorm(x, g, tm):
    m = x.shape[0]
    return pl.pallas_call(
        _rmsnorm_kernel,
        grid=(m // tm,),
        in_specs=[pl.BlockSpec((tm, D_MODEL), lambda i: (i, 0)),
                  pl.BlockSpec((1, D_MODEL), lambda i: (0, 0))],
        out_specs=pl.BlockSpec((tm, D_MODEL), lambda i: (i, 0)),
        out_shape=jax.ShapeDtypeStruct((m, D_MODEL), BF16),
        compiler_params=_params(("parallel",)),
        name="rmsnorm_in",
    )(x, g.reshape(1, D_MODEL))


def _proj_rope_kernel(x_ref, w_ref, cos_ref, sin_ref, *out_refs, scale, rope, tn, out_kinds):
    x = x_ref[...]
    for c in range(tn // V_DIM):
        acc = jnp.dot(x, w_ref[:, c * V_DIM:(c + 1) * V_DIM], preferred_element_type=F32)
        if rope:
            cs = cos_ref[...]
            sn = sin_ref[...]
            halves = []
            for m in range(2):
                a = acc[:, m * HEAD_DIM:(m + 1) * HEAD_DIM]
                halves.append(a * cs + pltpu.roll(a, HEAD_DIM // 2, axis=1) * sn)
            acc = jnp.concatenate(halves, axis=1)
        if scale != 1.0:
            acc = acc * scale
        for o_ref, kind in zip(out_refs, out_kinds):
            if kind == "head_transposed":
                o_ref[c] = acc.T.astype(o_ref.dtype)
            else:
                o_ref[:, c * V_DIM:(c + 1) * V_DIM] = acc.astype(o_ref.dtype)


def _proj_rope(xn, w_bf, col_block, cos_t, sin_t, *, tm, rope, scale, outs, name):
    m = xn.shape[0]
    tn = D_MODEL
    t_tiles = cos_t.shape[0] // tm
    kinds = tuple(k for k, _ in outs)
    kern = functools.partial(_proj_rope_kernel, scale=scale, rope=rope, tn=tn, out_kinds=kinds)
    out_specs, out_shape = [], []
    for kind, dt in outs:
        if kind == "head_transposed":
            out_specs.append(pl.BlockSpec((N_HEADS, None, V_DIM, tm), lambda i: (0, i, 0, 0)))
            out_shape.append(jax.ShapeDtypeStruct((N_HEADS, m // tm, V_DIM, tm), dt))
        else:
            out_specs.append(pl.BlockSpec((tm, tn), lambda i: (i, 0)))
            out_shape.append(jax.ShapeDtypeStruct((m, tn), dt))
    return pl.pallas_call(
        kern,
        grid=(m // tm,),
        in_specs=[pl.BlockSpec((tm, D_MODEL), lambda i: (i, 0)),
                  pl.BlockSpec((D_MODEL, tn), lambda i: (0, col_block)),
                  pl.BlockSpec((tm, HEAD_DIM), lambda i: (i % t_tiles, 0)),
                  pl.BlockSpec((tm, HEAD_DIM), lambda i: (i % t_tiles, 0))],
        out_specs=out_specs,
        out_shape=out_shape,
        compiler_params=_params(("parallel",)),
        name=name,
    )(xn, w_bf, cos_t, sin_t)


def _proj_gate_kernel(x_ref, w_ref, b_ref, o_ref, *, tn):
    x = x_ref[...]
    for c in range(tn // V_DIM):
        sl = slice(c * V_DIM, (c + 1) * V_DIM)
        acc = jnp.dot(x, w_ref[:, sl], preferred_element_type=F32) + b_ref[:, sl]
        o_ref[:, sl] = _sigmoid(acc).astype(o_ref.dtype)


def _proj_gate(xn, w_bf, col_block, b_gate, *, tm):
    m = xn.shape[0]
    tn = D_MODEL
    kern = functools.partial(_proj_gate_kernel, tn=tn)
    return pl.pallas_call(
        kern,
        grid=(2, m // tm),
        in_specs=[pl.BlockSpec((tm, D_MODEL), lambda j, i: (i, 0)),
                  pl.BlockSpec((D_MODEL, tn), lambda j, i: (0, col_block + j)),
                  pl.BlockSpec((1, tn), lambda j, i: (0, j))],
        out_specs=pl.BlockSpec((tm, tn), lambda j, i: (i, j)),
        out_shape=jax.ShapeDtypeStruct((m, 2 * D_MODEL), BF16),
        compiler_params=_params(("parallel", "parallel")),
        name="proj_gate",
    )(xn, w_bf, b_gate.reshape(1, 2 * D_MODEL))


def _odd_stride(seq_len):
    s = -(-seq_len // SUBLANES)
    return s if s % 2 == 1 else s + 1


def _proj_glu_conv_kernel(x_ref, w_ref, init_ref, wdw_ref, bdw_ref, dw_ref, state_ref,
                          pad_ref, out_scr, carry_ref, *, n_seq, seq_len, tiles_per_seq):
    tile = pl.program_id(1)
    stride = _odd_stride(seq_len)
    pad_rows = pad_ref.shape[2]
    group = next(g for g in (13, 9, 5, 3, 1) if stride % g == 0)
    x = x_ref[...]
    for q in range(GLU_CHUNKS):
        ls = slice(q * LANES, (q + 1) * LANES)
        ab = jnp.dot(x, w_ref[:, q * 2 * LANES:(q + 1) * 2 * LANES], preferred_element_type=F32)
        glu = ab[:, :LANES] * _sigmoid(ab[:, LANES:])
        for s in range(n_seq):
            if tiles_per_seq == 1:
                pad_ref[q, s, 0:HIST_ROWS, :] = init_ref[s, :, ls]
            else:
                first = (tile % tiles_per_seq) == 0
                pad_ref[q, s, 0:HIST_ROWS, :] = jnp.where(first, init_ref[s, :, ls], carry_ref[q])
            pad_ref[q, s, HIST_ROWS:HIST_ROWS + seq_len, :] = glu[s * seq_len:(s + 1) * seq_len, :]
            pad_ref[q, s, HIST_ROWS + seq_len:pad_rows, :] = jnp.zeros((pad_rows - HIST_ROWS - seq_len, LANES), F32)
            if tiles_per_seq != 1:
                carry_ref[q] = pad_ref[q, s, seq_len:seq_len + HIST_ROWS, :]
            state_ref[s, :, ls] = pad_ref[q, s, seq_len + HIST_OFF:seq_len + HIST_ROWS, :]

        bias = jnp.broadcast_to(bdw_ref[:, ls], (SUBLANES, LANES))
        for s in range(n_seq):
            for g0 in range(0, stride, group):
                accs = [bias] * group
                for j in range(CONV_K):
                    w = jnp.broadcast_to(wdw_ref[j:j + 1, ls], (SUBLANES, LANES))
                    for k in range(group):
                        rows = pl.ds(g0 + k + HIST_OFF + j, SUBLANES, stride=stride)
                        accs[k] = accs[k] + w * pad_ref[q, s, rows, :]
                for k in range(group):
                    out_scr[q, s, pl.ds(g0 + k, SUBLANES, stride=stride), :] = accs[k]
            dw_ref[s * seq_len:(s + 1) * seq_len, ls] = out_scr[q, s, 0:seq_len, :].astype(dw_ref.dtype)


def _proj_glu_conv(xn, w_glu, init_state, w_dw, b_dw, *, n_seq, seq_len, tiles_per_seq):
    m = xn.shape[0]
    tm = n_seq * seq_len
    uw = GLU_CHUNKS * LANES
    n_sequences = init_state.shape[0]
    stride = _odd_stride(seq_len)
    pad_rows = -(-(HIST_ROWS + SUBLANES * stride + SUBLANES) // SUBLANES) * SUBLANES
    kern = functools.partial(_proj_glu_conv_kernel, n_seq=n_seq, seq_len=seq_len, tiles_per_seq=tiles_per_seq)
    return pl.pallas_call(
        kern,
        grid=(C_CONV // uw, m // tm),
        in_specs=[pl.BlockSpec((tm, D_MODEL), lambda c, i: (i, 0)),
                  pl.BlockSpec((None, D_MODEL, 2 * uw), lambda c, i: (c, 0, 0)),
                  pl.BlockSpec((n_seq, HIST_ROWS, uw), lambda c, i: (i // tiles_per_seq, 0, c)),
                  pl.BlockSpec((CONV_K, uw), lambda c, i: (0, c)),
                  pl.BlockSpec((1, uw), lambda c, i: (0, c))],
        out_specs=[pl.BlockSpec((tm, uw), lambda c, i: (i, c)),
                   pl.BlockSpec((n_seq, CONV_K - 1, uw), lambda c, i: (i // tiles_per_seq, 0, c))],
        out_shape=[jax.ShapeDtypeStruct((m, C_CONV), BF16),
                   jax.ShapeDtypeStruct((n_sequences, CONV_K - 1, C_CONV), F32)],
        scratch_shapes=[pltpu.VMEM((GLU_CHUNKS, n_seq, pad_rows, LANES), F32),
                        pltpu.VMEM((GLU_CHUNKS, n_seq, SUBLANES * stride, LANES), F32),
                        pltpu.VMEM((GLU_CHUNKS, HIST_ROWS, LANES), F32)],
        compiler_params=_params(("parallel", "arbitrary")),
        name="proj_glu_conv",
    )(xn, w_glu, init_state, w_dw, b_dw.reshape(1, C_CONV))


def _lambda_value(lq_ref, lk_ref, lam_init):
    prod = lq_ref[...] * lk_ref[...]
    d = jnp.sum(prod, axis=1, keepdims=True)
    e = jnp.exp(d)
    return e[0:1, :] - e[1:2, :] + lam_init


def _subln_store(o_ref, acc0, l0, acc1, l1, lam, g_ref, lam_init):
    o = acc0 / l0 - lam * (acc1 / l1)
    ms = jnp.mean(o * o, axis=-1, keepdims=True)
    o = o * lax.rsqrt(ms + RMS_EPS) * g_ref[...] * (1.0 - lam_init)
    o_ref[...] = o.astype(o_ref.dtype)


def _nt_dot(a, b):
    return lax.dot_general(a, b, (((1,), (1,)), ((), ())), preferred_element_type=F32)


def _flash_prompt_kernel(lq_ref, lk_ref, g_ref, q_ref, k_ref, vt_ref, o_ref,
                         m_ref, l_ref, acc_ref, st_ref, p_ref, alpha_ref, *, tq, tk, lam_init):
    nd = tq // tk
    assert nd == 2
    qi = pl.program_id(2)
    q = q_ref[...]
    m_ref[...] = jnp.full(m_ref.shape, NEG_INF, F32)
    l_ref[...] = jnp.zeros(l_ref.shape, F32)
    acc_ref[...] = jnp.zeros(acc_ref.shape, F32)
    units = [(mp, c) for c in range(tq // Q_STRIP) for mp in range(2)]

    def scores(k, slot, diag, mp, c):
        sl = slice(mp * HEAD_DIM, (mp + 1) * HEAD_DIM)
        cs = slice(c * Q_STRIP, (c + 1) * Q_STRIP)
        st = _nt_dot(k[:, sl], q[cs, sl])
        if diag is not None:
            key_chunk = (lax.broadcasted_iota(jnp.int32, (tk, Q_STRIP), 0) + diag * tk) // CHUNK
            qry_chunk = (lax.broadcasted_iota(jnp.int32, (tk, Q_STRIP), 1) + c * Q_STRIP) // CHUNK
            st = jnp.where(key_chunk <= qry_chunk, st, NEG_INF)
        st_ref[slot, mp, :, cs] = st

    def softmax(slot, mp, c):
        cs = slice(c * Q_STRIP, (c + 1) * Q_STRIP)
        st = st_ref[slot, mp, :, cs]
        m_old = m_ref[mp, :, cs]
        m_new = jnp.maximum(m_old, jnp.max(st, axis=0, keepdims=True))
        alpha = jnp.exp2(m_old - m_new)
        p = jnp.exp2(st - m_new)
        l_ref[mp, :, cs] = l_ref[mp, :, cs] * alpha + jnp.sum(p, axis=0, keepdims=True)
        m_ref[mp, :, cs] = m_new
        p_ref[slot, mp, :, cs] = p.astype(BF16)
        alpha_ref[slot, mp, :, cs] = alpha

    def pv(vt, slot, mp, c):
        cs = slice(c * Q_STRIP, (c + 1) * Q_STRIP)
        acc_ref[mp, :, cs] = (acc_ref[mp, :, cs] * alpha_ref[slot, mp, :, cs]
                              + jnp.dot(vt, p_ref[slot, mp, :, cs], preferred_element_type=F32))

    def load_k(kidx):
        return k_ref[pl.ds(pl.multiple_of(kidx * tk, tk), tk), :]

    def key_of(i):
        return jnp.where(i < nd, nd * qi + i, i - nd)

    def iteration(s_slot, k_new, q_slot, vt, p_slot):
        for (mp, c) in units:
            softmax(s_slot, mp, c)
            scores(k_new, q_slot, None, mp, c)
            pv(vt, p_slot, mp, c)

    def softmax_all(slot):
        for (mp, c) in units:
            softmax(slot, mp, c)

    def pv_all(kidx, slot):
        vt = vt_ref[kidx]
        for (mp, c) in units:
            pv(vt, slot, mp, c)

    def masked_scores(slot, d):
        kd = load_k(nd * qi + d)
        for (mp, c) in units:
            scores(kd, slot, d, mp, c)

    masked_scores(0, 0)
    softmax_all(0)
    masked_scores(1, 1)

    def pair(jj, carry):
        j = 2 * jj + 2
        iteration(1, load_k(j - nd), 0, vt_ref[key_of(j - 2)], 0)
        iteration(0, load_k(j + 1 - nd), 1, vt_ref[key_of(j - 1)], 1)
        return carry

    lax.fori_loop(0, qi, pair, 0)
    n = nd * qi + nd
    softmax_all(1)
    pv_all(key_of(n - 2), 0)
    pv_all(key_of(n - 1), 1)

    lam = _lambda_value(lq_ref, lk_ref, lam_init)
    ot = acc_ref[0] * (1.0 / l_ref[0]) - lam * (acc_ref[1] * (1.0 / l_ref[1]))
    ms = jnp.mean(ot * ot, axis=0, keepdims=True)
    ot = ot * lax.rsqrt(ms + RMS_EPS) * (g_ref[...] * (1.0 - lam_init))
    o_ref[...] = ot.T.astype(o_ref.dtype)


def _flash_prompt(q, k, vt4, lam_q, lam_k, subln_g, *, batch, seq, tq, lam_init):
    tk = vt4.shape[-1]
    nkb = seq // tk
    q3 = q.reshape(batch, seq, D_MODEL)
    k3 = k.reshape(batch, seq, D_MODEL)
    kern = functools.partial(_flash_prompt_kernel, tq=tq, tk=tk, lam_init=lam_init)
    small = lambda shape: pl.BlockSpec(shape, lambda b, h, i: (0, 0))
    out = pl.pallas_call(
        kern,
        grid=(batch, N_HEADS, seq // tq),
        in_specs=[small((2, HEAD_DIM)), small((2, HEAD_DIM)), small((V_DIM, 1)),
                  pl.BlockSpec((None, tq, V_DIM), lambda b, h, i: (b, i, h)),
                  pl.BlockSpec((None, seq, V_DIM), lambda b, h, i: (b, 0, h)),
                  pl.BlockSpec((None, nkb, V_DIM, tk), lambda b, h, i: (h, b, 0, 0))],
        out_specs=pl.BlockSpec((None, tq, V_DIM), lambda b, h, i: (b, i, h)),
        out_shape=jax.ShapeDtypeStruct((batch, seq, D_MODEL), BF16),
        scratch_shapes=[pltpu.VMEM((2, 1, tq), F32), pltpu.VMEM((2, 1, tq), F32),
                        pltpu.VMEM((2, V_DIM, tq), F32), pltpu.VMEM((2, 2, tk, tq), F32),
                        pltpu.VMEM((2, 2, tk, tq), BF16), pltpu.VMEM((2, 2, 1, tq), F32)],
        compiler_params=_params(("parallel", "parallel", "arbitrary")),
        name="flash_prompt",
    )(lam_q, lam_k, subln_g.reshape(V_DIM, 1), q3, k3, vt4)
    return out.reshape(batch * seq, D_MODEL)


def _attn_sample_kernel(lq_ref, lk_ref, g_ref, q_ref, kn_ref, vn_ref, kc_ref, vc_ref, o_ref, *, lam_init):
    q = q_ref[...]
    kc = kc_ref[...].astype(BF16)
    vc = vc_ref[...].astype(BF16)
    kn = kn_ref[...]
    vn = vn_ref[...]
    accs, ls = [], []
    for mp in range(2):
        sl = slice(mp * HEAD_DIM, (mp + 1) * HEAD_DIM)
        s_past = _nt_dot(q[:, sl], kc[:, sl])
        s_new = _nt_dot(q[:, sl], kn[:, sl])
        mx = jnp.maximum(jnp.max(s_past, axis=-1, keepdims=True), jnp.max(s_new, axis=-1, keepdims=True))
        p_past = jnp.exp2(s_past - mx)
        p_new = jnp.exp2(s_new - mx)
        ls.append(jnp.sum(p_past, axis=-1, keepdims=True) + jnp.sum(p_new, axis=-1, keepdims=True))
        accs.append(jnp.dot(p_past.astype(BF16), vc, preferred_element_type=F32)
                    + jnp.dot(p_new.astype(BF16), vn, preferred_element_type=F32))
    lam = _lambda_value(lq_ref, lk_ref, lam_init)
    _subln_store(o_ref, accs[0], ls[0], accs[1], ls[1], lam, g_ref, lam_init)


def _attn_sample(q, k_new, v_new, cache_k, cache_v, lam_q, lam_k, subln_g, *, n_streams, dec_seq, past_len,
                 lam_init):
    kern = functools.partial(_attn_sample_kernel, lam_init=lam_init)
    small = lambda shape: pl.BlockSpec(shape, lambda b, h: (0, 0))
    new = pl.BlockSpec((dec_seq, V_DIM), lambda b, h: (b, h))
    past = pl.BlockSpec((None, past_len, V_DIM), lambda b, h: (b, 0, h))
    return pl.pallas_call(
        kern,
        grid=(n_streams, N_HEADS),
        in_specs=[small((2, HEAD_DIM)), small((2, HEAD_DIM)), small((1, V_DIM)), new, new, new, past, past],
        out_specs=new,
        out_shape=jax.ShapeDtypeStruct((n_streams * dec_seq, D_MODEL), BF16),
        compiler_params=_params(("parallel", "parallel")),
        name="attn_sample",
    )(lam_q, lam_k, subln_g.reshape(1, V_DIM), q, k_new, v_new, cache_k, cache_v)


def _attn_branch_kernel(o_ref, w_ref, g_ref, out_ref):
    o = o_ref[...]
    for c in range(D_MODEL // V_DIM):
        sl = slice(c * V_DIM, (c + 1) * V_DIM)
        acc = jnp.dot(o, w_ref[:, sl], preferred_element_type=F32)
        out_ref[:, sl] = (g_ref[:, sl].astype(F32) * acc).astype(out_ref.dtype)


def _conv_branch_kernel(dw_ref, lng_ref, lnb_ref, w_ref, g_ref, ga_ref, out_ref, act_ref):
    x = dw_ref[...].astype(F32)
    mu = jnp.mean(x, axis=-1, keepdims=True)
    xc = x - mu
    var = jnp.mean(xc * xc, axis=-1, keepdims=True)
    y = xc * lax.rsqrt(var + LN_EPS) * lng_ref[...] + lnb_ref[...]
    act_ref[...] = (y * _sigmoid(y)).astype(BF16)
    for c in range(D_MODEL // V_DIM):
        sl = slice(c * V_DIM, (c + 1) * V_DIM)
        acc = jnp.dot(act_ref[...], w_ref[:, sl], preferred_element_type=F32)
        merged = ga_ref[:, sl].astype(F32) + g_ref[:, sl].astype(F32) * acc
        out_ref[:, sl] = merged.astype(out_ref.dtype)


def _router_combine(logits):
    lane = lax.broadcasted_iota(jnp.int32, logits.shape, 1)
    big = jnp.int32(ROUTER_LANES)
    gmask = lane < N_GROUPS
    gl = jnp.where(gmask, logits, -jnp.inf)
    gmax = jnp.max(gl, axis=-1, keepdims=True)
    grp = jnp.min(jnp.where(gl == gmax, lane, big), axis=-1, keepdims=True)
    denom = jnp.sum(jnp.where(gmask, jnp.exp(gl - gmax), 0.0), axis=-1, keepdims=True)
    p_grp = 1.0 / denom
    eidx = lane - N_GROUPS
    emask = (eidx >= 0) & (eidx < N_EXPERTS) & ((eidx // EXPERTS_PER_GROUP) == grp)
    ev = jnp.where(emask, logits, -jnp.inf)
    l1 = jnp.max(ev, axis=-1, keepdims=True)
    i1 = jnp.min(jnp.where(ev == l1, lane, big), axis=-1, keepdims=True)
    ev2 = jnp.where(lane == i1, -jnp.inf, ev)
    l2 = jnp.max(ev2, axis=-1, keepdims=True)
    i2 = jnp.min(jnp.where(ev2 == l2, lane, big), axis=-1, keepdims=True)
    e2 = jnp.exp(l2 - l1)
    w1 = p_grp / (1.0 + e2)
    w2 = p_grp * e2 / (1.0 + e2)
    comb = jnp.where(lane == i1, w1, 0.0) + jnp.where(lane == i2, w2, 0.0)
    return jnp.where(lane == 0, grp.astype(F32), comb)


def _out_proj_kernel(mg_ref, w_ref, x_ref, gffn_ref, wr_ref, br_ref, h_ref, hn_ref, comb_ref):
    mg = mg_ref[...]
    for c in range(D_MODEL // V_DIM):
        sl = slice(c * V_DIM, (c + 1) * V_DIM)
        h_ref[:, sl] = x_ref[:, sl] + jnp.dot(mg, w_ref[:, sl], preferred_element_type=F32)
    h = h_ref[...]
    ms = jnp.mean(h * h, axis=-1, keepdims=True)
    hn = h * lax.rsqrt(ms + RMS_EPS) * gffn_ref[...]
    hn_hi = hn.astype(BF16)
    hn_ref[...] = hn.astype(hn_ref.dtype)
    hn_lo = (hn - hn_hi.astype(F32)).astype(BF16)
    hw = jnp.dot(hn_hi, wr_ref[...], preferred_element_type=F32)
    lw = jnp.dot(hn_lo, wr_ref[:, :ROUTER_LANES], preferred_element_type=F32)
    logits = hw[:, :ROUTER_LANES] + hw[:, ROUTER_LANES:] + lw + br_ref[...]
    comb_ref[...] = _router_combine(logits)


def _token_spec(tm, width=D_MODEL, col=0):
    return pl.BlockSpec((tm, width), lambda i: (i, col))


def _full_spec(shape):
    return pl.BlockSpec(shape, lambda i: (0,) * len(shape))


def _attn_branch(o, w_attn_bf, gates, *, tm):
    m = o.shape[0]
    return pl.pallas_call(
        _attn_branch_kernel,
        grid=(m // tm,),
        in_specs=[_token_spec(tm), _full_spec((ATTN_W, D_MODEL)), _token_spec(tm, D_MODEL, 0)],
        out_specs=_token_spec(tm),
        out_shape=jax.ShapeDtypeStruct((m, D_MODEL), BF16),
        compiler_params=_params(("parallel",)),
        name="attn_branch",
    )(o, w_attn_bf, gates)


def _conv_branch(dw, ln_g, ln_b, w_conv_bf, gates, ga, *, tm):
    m = dw.shape[0]
    return pl.pallas_call(
        _conv_branch_kernel,
        grid=(m // tm,),
        in_specs=[_token_spec(tm), _full_spec((1, C_CONV)), _full_spec((1, C_CONV)),
                  _full_spec((C_CONV, D_MODEL)), _token_spec(tm, D_MODEL, 1), _token_spec(tm)],
        out_specs=_token_spec(tm),
        out_shape=jax.ShapeDtypeStruct((m, D_MODEL), BF16),
        scratch_shapes=[pltpu.VMEM((tm, C_CONV), BF16)],
        compiler_params=_params(("parallel",)),
        name="conv_branch",
    )(dw, ln_g.reshape(1, C_CONV), ln_b.reshape(1, C_CONV), w_conv_bf, gates, ga)


def _out_proj(merged, w_out_bf, x, g_ffn, w_router, b_router, *, tm, hn_dtype):
    m = merged.shape[0]
    return pl.pallas_call(
        _out_proj_kernel,
        grid=(m // tm,),
        in_specs=[_token_spec(tm), _full_spec((D_MODEL, D_MODEL)), _token_spec(tm), _full_spec((1, D_MODEL)),
                  _full_spec((D_MODEL, 2 * ROUTER_LANES)), _full_spec((1, ROUTER_LANES))],
        out_specs=[_token_spec(tm), _token_spec(tm), _token_spec(tm, ROUTER_LANES)],
        out_shape=[jax.ShapeDtypeStruct((m, D_MODEL), F32), jax.ShapeDtypeStruct((m, D_MODEL), hn_dtype),
                   jax.ShapeDtypeStruct((m, ROUTER_LANES), F32)],
        compiler_params=_params(("parallel",)),
        name="out_proj_router",
    )(merged, w_out_bf, x, g_ffn.reshape(1, D_MODEL), w_router, b_router)


def _moe_dense_kernel(hn_ref, h_ref, comb_ref, wg_ref, wu_ref, wd_ref, gfin_ref, y_ref, acc_ref):
    e = pl.program_id(1)

    @pl.when(e == 0)
    def _():
        acc_ref[...] = h_ref[...]

    x = hn_ref[...]
    gate = jnp.dot(x, wg_ref[...], preferred_element_type=F32)
    up = jnp.dot(x, wu_ref[...], preferred_element_type=F32)
    comb = comb_ref[...]
    lane = lax.broadcasted_iota(jnp.int32, comb.shape, 1)
    c = jnp.sum(jnp.where(lane == e + N_GROUPS, comb, 0.0), axis=-1, keepdims=True)
    hid = (gate * _sigmoid(gate) * up * c).astype(BF16)
    acc_ref[...] += jnp.dot(hid, wd_ref[...], preferred_element_type=F32)

    @pl.when(e == N_EXPERTS - 1)
    def _():
        y = acc_ref[...]
        ms = jnp.mean(y * y, axis=-1, keepdims=True)
        y_ref[...] = y * lax.rsqrt(ms + RMS_EPS) * gfin_ref[...]


def _moe_dense(hn, h, comb, wg_bf, wu_bf, wd_bf, g_final, *, tm):
    m = hn.shape[0]
    return pl.pallas_call(
        _moe_dense_kernel,
        grid=(m // tm, N_EXPERTS),
        in_specs=[pl.BlockSpec((tm, D_MODEL), lambda i, e: (i, 0)),
                  pl.BlockSpec((tm, D_MODEL), lambda i, e: (i, 0)),
                  pl.BlockSpec((tm, ROUTER_LANES), lambda i, e: (i, 0)),
                  pl.BlockSpec((None, D_MODEL, D_EXPERT), lambda i, e: (e, 0, 0)),
                  pl.BlockSpec((None, D_MODEL, D_EXPERT), lambda i, e: (e, 0, 0)),
                  pl.BlockSpec((None, D_EXPERT, D_MODEL), lambda i, e: (e, 0, 0)),
                  pl.BlockSpec((1, D_MODEL), lambda i, e: (0, 0))],
        out_specs=pl.BlockSpec((tm, D_MODEL), lambda i, e: (i, 0)),
        out_shape=jax.ShapeDtypeStruct((m, D_MODEL), F32),
        scratch_shapes=[pltpu.VMEM((tm, D_MODEL), F32)],
        compiler_params=_params(("parallel", "arbitrary")),
        name="moe_dense",
    )(hn, h, comb, wg_bf, wu_bf, wd_bf, g_final.reshape(1, D_MODEL))


ATTN_W = N_HEADS * V_DIM


def _gather_rows_kernel(idx_ref, a_hbm, b_hbm, a_out, b_out, sem_a, sem_b, *, rows):
    base = pl.program_id(0) * rows

    def issue(r, carry):
        tok = idx_ref[base + r]
        pltpu.make_async_copy(a_hbm.at[pl.ds(tok, 1)], a_out.at[pl.ds(r, 1)], sem_a).start()
        pltpu.make_async_copy(b_hbm.at[pl.ds(tok, 1)], b_out.at[pl.ds(r, 1)], sem_b).start()
        return carry

    lax.fori_loop(0, rows, issue, 0, unroll=8)
    pltpu.make_async_copy(a_hbm.at[pl.ds(0, rows)], a_out, sem_a).wait()
    pltpu.make_async_copy(b_hbm.at[pl.ds(0, rows)], b_out, sem_b).wait()


def _gather_rows(idx, a, b, *, rows):
    n = idx.shape[0]
    grid_spec = pltpu.PrefetchScalarGridSpec(
        num_scalar_prefetch=1,
        grid=(n // rows,),
        in_specs=[pl.BlockSpec(memory_space=pl.ANY), pl.BlockSpec(memory_space=pl.ANY)],
        out_specs=[pl.BlockSpec((rows, a.shape[1]), lambda i, idx: (i, 0)),
                   pl.BlockSpec((rows, b.shape[1]), lambda i, idx: (i, 0))],
        scratch_shapes=[pltpu.SemaphoreType.DMA, pltpu.SemaphoreType.DMA],
    )
    return pl.pallas_call(
        functools.partial(_gather_rows_kernel, rows=rows),
        grid_spec=grid_spec,
        out_shape=[jax.ShapeDtypeStruct((n, a.shape[1]), a.dtype),
                   jax.ShapeDtypeStruct((n, b.shape[1]), b.dtype)],
        compiler_params=_params(("arbitrary",)),
        name="moe_gather",
    )(idx, a, b)


def _moe_group_kernel(tg_ref, x_ref, comb_ref, wg_ref, wu_ref, wd_ref, y_ref, xb_ref):
    i = pl.program_id(0)
    e = pl.program_id(1)

    @pl.when(e == 0)
    def _():
        xb_ref[...] = x_ref[...].astype(BF16)
        y_ref[...] = jnp.zeros(y_ref.shape, F32)

    x = xb_ref[...]
    gate = jnp.dot(x, wg_ref[...], preferred_element_type=F32)
    up = jnp.dot(x, wu_ref[...], preferred_element_type=F32)
    comb = comb_ref[...]
    lane = lax.broadcasted_iota(jnp.int32, comb.shape, 1)
    expert_lane = N_GROUPS + tg_ref[i] * EXPERTS_PER_GROUP + e
    c = jnp.sum(jnp.where(lane == expert_lane, comb, 0.0), axis=-1, keepdims=True)
    hid = (gate * _sigmoid(gate) * up * c).astype(BF16)
    y_ref[...] += jnp.dot(hid, wd_ref[...], preferred_element_type=F32)


def _moe_group(tile_group, x_sorted, comb_sorted, wg_bf, wu_bf, wd_bf, *, rows):
    p = x_sorted.shape[0]
    w_idx = lambda i, e, tg: (tg[i] * EXPERTS_PER_GROUP + e, 0, 0)
    grid_spec = pltpu.PrefetchScalarGridSpec(
        num_scalar_prefetch=1,
        grid=(p // rows, EXPERTS_PER_GROUP),
        in_specs=[pl.BlockSpec((rows, D_MODEL), lambda i, e, tg: (i, 0)),
                  pl.BlockSpec((rows, ROUTER_LANES), lambda i, e, tg: (i, 0)),
                  pl.BlockSpec((None, D_MODEL, D_EXPERT), w_idx),
                  pl.BlockSpec((None, D_MODEL, D_EXPERT), w_idx),
                  pl.BlockSpec((None, D_EXPERT, D_MODEL), w_idx)],
        out_specs=pl.BlockSpec((rows, D_MODEL), lambda i, e, tg: (i, 0)),
        scratch_shapes=[pltpu.VMEM((rows, D_MODEL), BF16)],
    )
    return pl.pallas_call(
        _moe_group_kernel,
        grid_spec=grid_spec,
        out_shape=jax.ShapeDtypeStruct((p, D_MODEL), F32),
        compiler_params=_params(("arbitrary", "arbitrary")),
        name="moe_group",
    )(tile_group, x_sorted, comb_sorted, wg_bf, wu_bf, wd_bf)


def _moe_finish_kernel(dest_ref, ys_hbm, h_ref, g_ref, y_ref, buf_ref, sem, *, rows):
    base = pl.program_id(0) * rows

    def issue(r, carry):
        p = dest_ref[base + r]
        pltpu.make_async_copy(ys_hbm.at[pl.ds(p, 1)], buf_ref.at[pl.ds(r, 1)], sem).start()
        return carry

    lax.fori_loop(0, rows, issue, 0, unroll=8)
    pltpu.make_async_copy(ys_hbm.at[pl.ds(0, rows)], buf_ref, sem).wait()
    y = h_ref[...] + buf_ref[...]
    ms = jnp.mean(y * y, axis=-1, keepdims=True)
    y_ref[...] = y * lax.rsqrt(ms + RMS_EPS) * g_ref[...]


def _moe_finish(dest, y_sorted, h, g_final, *, rows):
    m = h.shape[0]
    grid_spec = pltpu.PrefetchScalarGridSpec(
        num_scalar_prefetch=1,
        grid=(m // rows,),
        in_specs=[pl.BlockSpec(memory_space=pl.ANY),
                  pl.BlockSpec((rows, D_MODEL), lambda i, d: (i, 0)),
                  pl.BlockSpec((1, D_MODEL), lambda i, d: (0, 0))],
        out_specs=pl.BlockSpec((rows, D_MODEL), lambda i, d: (i, 0)),
        scratch_shapes=[pltpu.VMEM((rows, D_MODEL), F32), pltpu.SemaphoreType.DMA],
    )
    return pl.pallas_call(
        functools.partial(_moe_finish_kernel, rows=rows),
        grid_spec=grid_spec,
        out_shape=jax.ShapeDtypeStruct((m, D_MODEL), F32),
        compiler_params=_params(("arbitrary",)),
        name="moe_finish",
    )(dest, y_sorted, h, g_final.reshape(1, D_MODEL))


def _moe_routed(hn, h, comb, wg_bf, wu_bf, wd_bf, g_final, *, rows):
    m = hn.shape[0]
    grp = comb[:, 0].astype(jnp.int32)
    onehot = (grp[:, None] == jnp.arange(N_GROUPS, dtype=jnp.int32)[None, :]).astype(jnp.int32)
    rank = jnp.cumsum(onehot, axis=0) - onehot
    counts = jnp.sum(onehot, axis=0)
    padded = ((counts + rows - 1) // rows) * rows
    seg_end = jnp.cumsum(padded)
    seg_start = seg_end - padded
    dest = jnp.sum(onehot * (seg_start[None, :] + rank), axis=1).astype(jnp.int32)
    p_rows = m + N_GROUPS * rows
    src = jnp.zeros((p_rows,), jnp.int32).at[dest].set(jnp.arange(m, dtype=jnp.int32))
    tile_start = jnp.arange(p_rows // rows, dtype=jnp.int32) * rows
    tile_group = jnp.minimum(jnp.sum((tile_start[:, None] >= seg_end[None, :]).astype(jnp.int32), axis=1),
                             N_GROUPS - 1).astype(jnp.int32)
    x_sorted, comb_sorted = _gather_rows(src, hn, comb, rows=rows)
    y_sorted = _moe_group(tile_group, x_sorted, comb_sorted, wg_bf, wu_bf, wd_bf, rows=rows)
    return _moe_finish(dest, y_sorted, h, g_final, rows=rows)


def _rope_tables(pos):
    half = HEAD_DIM // 2
    inv_freq = ROPE_THETA ** (-jnp.arange(half, dtype=F32) / half)
    ang = pos.astype(F32)[:, None] * inv_freq[None, :]
    cos, sin = jnp.cos(ang), jnp.sin(ang)
    return jnp.concatenate([cos, cos], axis=-1), jnp.concatenate([-sin, sin], axis=-1)


def _layer_tokens(x, pos_tables, init_state, attn_fn, wts, *, tm, n_seq, seq_len, tiles_per_seq, v_kind, routed):
    cos_t, sin_t = pos_tables
    xn = _rmsnorm(x, wts["norm_mix_g"], tm)
    (q,) = _proj_rope(xn, wts["w_in"], 0, cos_t, sin_t, tm=tm, rope=True, scale=SCALE * LOG2_E,
                      outs=(("tokens", BF16),), name="proj_q")
    k_f32, k_bf = _proj_rope(xn, wts["w_in"], 1, cos_t, sin_t, tm=tm, rope=True, scale=1.0,
                             outs=(("tokens", F32), ("tokens", BF16)), name="proj_k")
    v_f32, v_bf = _proj_rope(xn, wts["w_in"], 2, cos_t, sin_t, tm=tm, rope=False, scale=1.0,
                             outs=(("tokens", F32), (v_kind, BF16)), name="proj_v")
    dw, new_conv = _proj_glu_conv(xn, wts["w_glu"], init_state, wts["w_dw"], wts["b_dw"],
                                  n_seq=n_seq, seq_len=seq_len, tiles_per_seq=tiles_per_seq)
    gates = _proj_gate(xn, wts["w_in"], 5, wts["b_gate"], tm=tm)
    o = attn_fn(q, k_bf, v_bf)
    ga = _attn_branch(o, wts["w_attn"], gates, tm=tm)
    merged = _conv_branch(dw, wts["conv_ln_g"], wts["conv_ln_b"], wts["w_conv"], gates, ga, tm=tm)
    h, hn, comb = _out_proj(merged, wts["w_out"], x, wts["norm_ffn_g"], wts["w_router"], wts["b_router"], tm=tm,
                            hn_dtype=F32 if routed else BF16)
    moe = functools.partial(_moe_routed, rows=MOE_ROWS) if routed else functools.partial(_moe_dense, tm=tm)
    y = moe(hn, h, comb, wts["w_eg"], wts["w_eu"], wts["w_ed"], wts["norm_final_g"])
    return y, k_f32, v_f32, new_conv


def kernel(x_prompt, x_sample, cache_k, cache_v, state_conv, norm_mix_g, w_in, b_gate, lambda_q, lambda_k, subln_g, w_attn_branch, w_dw, b_dw, conv_ln_g, conv_ln_b, w_conv_branch, w_out, norm_ffn_g, w_router_group, b_router_group, w_router_expert, b_router_expert, w_exp_gate, w_exp_up, w_exp_down, norm_final_g):
    batch, seq, _ = x_prompt.shape
    n_streams, dec_seq, _ = x_sample.shape
    depth, _, past_len = cache_k.shape[:3]
    assert depth == 1
    assert past_len % CHUNK == 0 and dec_seq <= CHUNK
    l = 0
    lam_init = 0.8 - 0.6 * math.exp(-0.3 * l)

    pad_r = ROUTER_LANES - N_GROUPS - N_EXPERTS
    w_router = jnp.concatenate([w_router_group[l], w_router_expert[l], jnp.zeros((D_MODEL, pad_r), F32)], axis=1)
    b_router = jnp.concatenate([b_router_group[l], b_router_expert[l], jnp.zeros((pad_r,), F32)]).reshape(1, -1)
    w_router_hi = w_router.astype(BF16)
    w_router_lo = (w_router - w_router_hi.astype(F32)).astype(BF16)
    w_router = jnp.concatenate([w_router_hi, w_router_lo], axis=1)
    w_in_bf = w_in[l].astype(BF16)
    n_chunks = C_CONV // LANES
    glu_cols = N_HEADS * (4 * HEAD_DIM + V_DIM)
    w_val = w_in_bf[:, glu_cols:glu_cols + C_CONV].reshape(D_MODEL, n_chunks, 1, LANES)
    w_gat = w_in_bf[:, glu_cols + C_CONV:glu_cols + 2 * C_CONV].reshape(D_MODEL, n_chunks, 1, LANES)
    w_glu = jnp.concatenate([w_val, w_gat], axis=2).reshape(D_MODEL, n_chunks // GLU_CHUNKS, GLU_CHUNKS * 2 * LANES)
    w_glu = w_glu.transpose(1, 0, 2)
    wts = dict(
        norm_mix_g=norm_mix_g[l], w_in=w_in_bf, w_glu=w_glu, b_gate=b_gate[l],
        w_attn=w_attn_branch[l].astype(BF16), w_dw=w_dw[l], b_dw=b_dw[l],
        conv_ln_g=conv_ln_g[l], conv_ln_b=conv_ln_b[l], w_conv=w_conv_branch[l].astype(BF16),
        w_out=w_out[l].astype(BF16), norm_ffn_g=norm_ffn_g[l], w_router=w_router, b_router=b_router,
        w_eg=w_exp_gate[l].astype(BF16), w_eu=w_exp_up[l].astype(BF16), w_ed=w_exp_down[l].astype(BF16),
        norm_final_g=norm_final_g)
    lam_q, lam_k, sub_g = lambda_q[l], lambda_k[l], subln_g[l]

    tm = 512
    tabs_p = _rope_tables(jnp.arange(seq, dtype=jnp.int32))
    init_p = jnp.zeros((batch, HIST_ROWS, C_CONV), F32)
    attn_p = functools.partial(_flash_prompt, lam_q=lam_q, lam_k=lam_k, subln_g=sub_g,
                               batch=batch, seq=seq, tq=2 * tm, lam_init=lam_init)
    y_p, k_p, v_p, conv_p = _layer_tokens(
        x_prompt.reshape(batch * seq, D_MODEL), tabs_p, init_p, attn_p, wts,
        tm=tm, n_seq=1, seq_len=tm, tiles_per_seq=seq // tm, v_kind="head_transposed", routed=True)

    cos_s, sin_s = _rope_tables(past_len + jnp.arange(dec_seq, dtype=jnp.int32))
    tabs_s = (jnp.tile(cos_s, (n_streams, 1)), jnp.tile(sin_s, (n_streams, 1)))
    init_s = jnp.pad(state_conv[l], ((0, 0), (HIST_OFF, 0), (0, 0)))
    ck = cache_k[l].reshape(n_streams, past_len, D_MODEL)
    cv = cache_v[l].reshape(n_streams, past_len, D_MODEL)
    attn_s = functools.partial(_attn_sample, cache_k=ck, cache_v=cv, lam_q=lam_q, lam_k=lam_k, subln_g=sub_g,
                               n_streams=n_streams, dec_seq=dec_seq, past_len=past_len, lam_init=lam_init)
    m_s = n_streams * dec_seq
    y_s, k_s, v_s, conv_s = _layer_tokens(
        x_sample.reshape(m_s, D_MODEL), tabs_s, init_s, attn_s, wts,
        tm=m_s, n_seq=n_streams, seq_len=dec_seq, tiles_per_seq=1, v_kind="tokens", routed=False)

    return (y_p.reshape(batch, seq, D_MODEL),
            y_s.reshape(n_streams, dec_seq, D_MODEL),
            k_p.reshape(1, batch, seq, N_HEADS, 2, HEAD_DIM),
            v_p.reshape(1, batch, seq, N_HEADS, V_DIM),
            conv_p.reshape(1, batch, CONV_K - 1, C_CONV),
            k_s.reshape(1, n_streams, dec_seq, N_HEADS, 2, HEAD_DIM),
            v_s.reshape(1, n_streams, dec_seq, N_HEADS, V_DIM),
            conv_s.reshape(1, n_streams, CONV_K - 1, C_CONV))
```

```python
import functools
import math

import jax
import jax.numpy as jnp
from jax import lax
from jax.experimental import pallas as pl
from jax.experimental.pallas import tpu as pltpu

F32 = jnp.float32
BF16 = jnp.bfloat16

D_MODEL = 2048
CHUNK = 64
HEAD_DIM = 128
N_HEADS = D_MODEL // (2 * HEAD_DIM)
V_DIM = 2 * HEAD_DIM
C_CONV = D_MODEL
CONV_K = 31
N_GROUPS = 4
EXPERTS_PER_GROUP = 4
N_EXPERTS = N_GROUPS * EXPERTS_PER_GROUP
D_EXPERT = D_MODEL // 4
ROPE_THETA = 10000.0
RMS_EPS = 1e-6
LN_EPS = 1e-5
NEG_INF = -1e30
SCALE = HEAD_DIM ** -0.5
LOG2_E = math.log2(math.e)
Q_STRIP = 256

LANES = 128
SUBLANES = 8
GLU_CHUNKS = 4
MOE_ROWS = 512
HIST_ROWS = 32
HIST_OFF = HIST_ROWS - (CONV_K - 1)
ROUTER_LANES = 128
VMEM_LIMIT = 56 * 1024 * 1024


def _params(sem):
    return pltpu.CompilerParams(dimension_semantics=sem, vmem_limit_bytes=VMEM_LIMIT)


def _sigmoid(x):
    return 1.0 / (1.0 + jnp.exp(-x))


def _rmsnorm_kernel(x_ref, g_ref, o_ref):
    x = x_ref[...]
    ms = jnp.mean(x * x, axis=-1, keepdims=True)
    o_ref[...] = (x * lax.rsqrt(ms + RMS_EPS) * g_ref[...]).astype(o_ref.dtype)


def _rmsnorm(x, g, tm):
    m = x.shape[0]
    return pl.pallas_call(
        _rmsnorm_kernel,
        grid=(m // tm,),
        in_specs=[pl.BlockSpec((tm, D_MODEL), lambda i: (i, 0)),
                  pl.BlockSpec((1, D_MODEL), lambda i: (0, 0))],
        out_specs=pl.BlockSpec((tm, D_MODEL), lambda i: (i, 0)),
        out_shape=jax.ShapeDtypeStruct((m, D_MODEL), BF16),
        compiler_params=_params(("parallel",)),
        name="rmsnorm_in",
    )(x, g.reshape(1, D_MODEL))


def _proj_rope_kernel(x_ref, w_ref, cos_ref, sin_ref, *out_refs, scale, rope, tn, out_kinds):
    x = x_ref[...]
    for c in range(tn // V_DIM):
        acc = jnp.dot(x, w_ref[:, c * V_DIM:(c + 1) * V_DIM], preferred_element_type=F32)
        if rope:
            cs = cos_ref[...]
            sn = sin_ref[...]
            halves = []
            for m in range(2):
                a = acc[:, m * HEAD_DIM:(m + 1) * HEAD_DIM]
                halves.append(a * cs + pltpu.roll(a, HEAD_DIM // 2, axis=1) * sn)
            acc = jnp.concatenate(halves, axis=1)
        if scale != 1.0:
            acc = acc * scale
        for o_ref, kind in zip(out_refs, out_kinds):
            if kind == "head_transposed":
                o_ref[c] = acc.T.astype(o_ref.dtype)
            else:
                o_ref[:, c * V_DIM:(c + 1) * V_DIM] = acc.astype(o_ref.dtype)


def _proj_rope(xn, w_bf, col_block, cos_t, sin_t, *, tm, rope, scale, outs, name):
    m = xn.shape[0]
    tn = D_MODEL
    t_tiles = cos_t.shape[0] // tm
    kinds = tuple(k for k, _ in outs)
    kern = functools.partial(_proj_rope_kernel, scale=scale, rope=rope, tn=tn, out_kinds=kinds)
    out_specs, out_shape = [], []
    for kind, dt in outs:
        if kind == "head_transposed":
            out_specs.append(pl.BlockSpec((N_HEADS, None, V_DIM, tm), lambda i: (0, i, 0, 0)))
            out_shape.append(jax.ShapeDtypeStruct((N_HEADS, m // tm, V_DIM, tm), dt))
        else:
            out_specs.append(pl.BlockSpec((tm, tn), lambda i: (i, 0)))
            out_shape.append(jax.ShapeDtypeStruct((m, tn), dt))
    return pl.pallas_call(
        kern,
        grid=(m // tm,),
        in_specs=[pl.BlockSpec((tm, D_MODEL), lambda i: (i, 0)),
                  pl.BlockSpec((D_MODEL, tn), lambda i: (0, col_block)),
                  pl.BlockSpec((tm, HEAD_DIM), lambda i: (i % t_tiles, 0)),
                  pl.BlockSpec((tm, HEAD_DIM), lambda i: (i % t_tiles, 0))],
        out_specs=out_specs,
        out_shape=out_shape,
        compiler_params=_params(("parallel",)),
        name=name,
    )(xn, w_bf, cos_t, sin_t)


def _proj_gate_kernel(x_ref, w_ref, b_ref, o_ref, *, tn):
    x = x_ref[...]
    for c in range(tn // V_DIM):
        sl = slice(c * V_DIM, (c + 1) * V_DIM)
        acc = jnp.dot(x, w_ref[:, sl], preferred_element_type=F32) + b_ref[:, sl]
        o_ref[:, sl] = _sigmoid(acc).astype(o_ref.dtype)


def _proj_gate(xn, w_bf, col_block, b_gate, *, tm):
    m = xn.shape[0]
    tn = D_MODEL
    kern = functools.partial(_proj_gate_kernel, tn=tn)
    return pl.pallas_call(
        kern,
        grid=(2, m // tm),
        in_specs=[pl.BlockSpec((tm, D_MODEL), lambda j, i: (i, 0)),
                  pl.BlockSpec((D_MODEL, tn), lambda j, i: (0, col_block + j)),
                  pl.BlockSpec((1, tn), lambda j, i: (0, j))],
        out_specs=pl.BlockSpec((tm, tn), lambda j, i: (i, j)),
        out_shape=jax.ShapeDtypeStruct((m, 2 * D_MODEL), BF16),
        compiler_params=_params(("parallel", "parallel")),
        name="proj_gate",
    )(xn, w_bf, b_gate.reshape(1, 2 * D_MODEL))


def _odd_stride(seq_len):
    s = -(-seq_len // SUBLANES)
    return s if s % 2 == 1 else s + 1


def _proj_glu_conv_kernel(x_ref, w_ref, init_ref, wdw_ref, bdw_ref, dw_ref, state_ref,
                          pad_ref, out_scr, carry_ref, *, n_seq, seq_len, tiles_per_seq):
    tile = pl.program_id(1)
    stride = _odd_stride(seq_len)
    pad_rows = pad_ref.shape[2]
    group = next(g for g in (13, 9, 5, 3, 1) if stride % g == 0)
    x = x_ref[...]
    for q in range(GLU_CHUNKS):
        ls = slice(q * LANES, (q + 1) * LANES)
        ab = jnp.dot(x, w_ref[:, q * 2 * LANES:(q + 1) * 2 * LANES], preferred_element_type=F32)
        glu = ab[:, :LANES] * _sigmoid(ab[:, LANES:])
        for s in range(n_seq):
            if tiles_per_seq == 1:
                pad_ref[q, s, 0:HIST_ROWS, :] = init_ref[s, :, ls]
            else:
                first = (tile % tiles_per_seq) == 0
                pad_ref[q, s, 0:HIST_ROWS, :] = jnp.where(first, init_ref[s, :, ls], carry_ref[q])
            pad_ref[q, s, HIST_ROWS:HIST_ROWS + seq_len, :] = glu[s * seq_len:(s + 1) * seq_len, :]
            pad_ref[q, s, HIST_ROWS + seq_len:pad_rows, :] = jnp.zeros((pad_rows - HIST_ROWS - seq_len, LANES), F32)
            if tiles_per_seq != 1:
                carry_ref[q] = pad_ref[q, s, seq_len:seq_len + HIST_ROWS, :]
            state_ref[s, :, ls] = pad_ref[q, s, seq_len + HIST_OFF:seq_len + HIST_ROWS, :]

        bias = jnp.broadcast_to(bdw_ref[:, ls], (SUBLANES, LANES))
        for s in range(n_seq):
            for g0 in range(0, stride, group):
                accs = [bias] * group
                for j in range(CONV_K):
                    w = jnp.broadcast_to(wdw_ref[j:j + 1, ls], (SUBLANES, LANES))
                    for k in range(group):
                        rows = pl.ds(g0 + k + HIST_OFF + j, SUBLANES, stride=stride)
                        accs[k] = accs[k] + w * pad_ref[q, s, rows, :]
                for k in range(group):
                    out_scr[q, s, pl.ds(g0 + k, SUBLANES, stride=stride), :] = accs[k]
            dw_ref[s * seq_len:(s + 1) * seq_len, ls] = out_scr[q, s, 0:seq_len, :].astype(dw_ref.dtype)


def _proj_glu_conv(xn, w_glu, init_state, w_dw, b_dw, *, n_seq, seq_len, tiles_per_seq):
    m = xn.shape[0]
    tm = n_seq * seq_len
    uw = GLU_CHUNKS * LANES
    n_sequences = init_state.shape[0]
    stride = _odd_stride(seq_len)
    pad_rows = -(-(HIST_ROWS + SUBLANES * stride + SUBLANES) // SUBLANES) * SUBLANES
    kern = functools.partial(_proj_glu_conv_kernel, n_seq=n_seq, seq_len=seq_len, tiles_per_seq=tiles_per_seq)
    return pl.pallas_call(
        kern,
        grid=(C_CONV // uw, m // tm),
        in_specs=[pl.BlockSpec((tm, D_MODEL), lambda c, i: (i, 0)),
                  pl.BlockSpec((None, D_MODEL, 2 * uw), lambda c, i: (c, 0, 0)),
                  pl.BlockSpec((n_seq, HIST_ROWS, uw), lambda c, i: (i // tiles_per_seq, 0, c)),
                  pl.BlockSpec((CONV_K, uw), lambda c, i: (0, c)),
                  pl.BlockSpec((1, uw), lambda c, i: (0, c))],
        out_specs=[pl.BlockSpec((tm, uw), lambda c, i: (i, c)),
                   pl.BlockSpec((n_seq, CONV_K - 1, uw), lambda c, i: (i // tiles_per_seq, 0, c))],
        out_shape=[jax.ShapeDtypeStruct((m, C_CONV), BF16),
                   jax.ShapeDtypeStruct((n_sequences, CONV_K - 1, C_CONV), F32)],
        scratch_shapes=[pltpu.VMEM((GLU_CHUNKS, n_seq, pad_rows, LANES), F32),
                        pltpu.VMEM((GLU_CHUNKS, n_seq, SUBLANES * stride, LANES), F32),
                        pltpu.VMEM((GLU_CHUNKS, HIST_ROWS, LANES), F32)],
        compiler_params=_params(("parallel", "arbitrary")),
        name="proj_glu_conv",
    )(xn, w_glu, init_state, w_dw, b_dw.reshape(1, C_CONV))


def _lambda_value(lq_ref, lk_ref, lam_init):
    prod = lq_ref[...] * lk_ref[...]
    d = jnp.sum(prod, axis=1, keepdims=True)
    e = jnp.exp(d)
    return e[0:1, :] - e[1:2, :] + lam_init


def _subln_store(o_ref, acc0, l0, acc1, l1, lam, g_ref, lam_init):
    o = acc0 / l0 - lam * (acc1 / l1)
    ms = jnp.mean(o * o, axis=-1, keepdims=True)
    o = o * lax.rsqrt(ms + RMS_EPS) * g_ref[...] * (1.0 - lam_init)
    o_ref[...] = o.astype(o_ref.dtype)


def _nt_dot(a, b):
    return lax.dot_general(a, b, (((1,), (1,)), ((), ())), preferred_element_type=F32)


def _flash_prompt_kernel(lq_ref, lk_ref, g_ref, q_ref, k_ref, vt_ref, o_ref,
                         m_ref, l_ref, acc_ref, st_ref, p_ref, alpha_ref, *, tq, tk, lam_init):
    nd = tq // tk
    assert nd == 2
    qi = pl.program_id(2)
    q = q_ref[...]
    m_ref[...] = jnp.full(m_ref.shape, NEG_INF, F32)
    l_ref[...] = jnp.zeros(l_ref.shape, F32)
    acc_ref[...] = jnp.zeros(acc_ref.shape, F32)
    units = [(mp, c) for c in range(tq // Q_STRIP) for mp in range(2)]

    def scores(k, slot, diag, mp, c):
        sl = slice(mp * HEAD_DIM, (mp + 1) * HEAD_DIM)
        cs = slice(c * Q_STRIP, (c + 1) * Q_STRIP)
        st = _nt_dot(k[:, sl], q[cs, sl])
        if diag is not None:
            key_chunk = (lax.broadcasted_iota(jnp.int32, (tk, Q_STRIP), 0) + diag * tk) // CHUNK
            qry_chunk = (lax.broadcasted_iota(jnp.int32, (tk, Q_STRIP), 1) + c * Q_STRIP) // CHUNK
            st = jnp.where(key_chunk <= qry_chunk, st, NEG_INF)
        st_ref[slot, mp, c] = st

    def softmax(slot, mp, c):
        st = st_ref[slot, mp, c]
        m_old = m_ref[mp, c]
        m_new = jnp.maximum(m_old, jnp.max(st, axis=0, keepdims=True))
        alpha = jnp.exp2(m_old - m_new)
        p = jnp.exp2(st - m_new)
        l_ref[mp, c] = l_ref[mp, c] * alpha + jnp.sum(p, axis=0, keepdims=True)
        m_ref[mp, c] = m_new
        p_ref[slot, mp, c] = p.astype(BF16)
        alpha_ref[slot, mp, c] = alpha

    def pv(vt, slot, mp, c):
        acc_ref[mp, c] = (acc_ref[mp, c] * alpha_ref[slot, mp, c]
                          + jnp.dot(vt, p_ref[slot, mp, c], preferred_element_type=F32))

    def load_k(kidx):
        return k_ref[pl.ds(pl.multiple_of(kidx * tk, tk), tk), :]

    def key_of(i):
        return jnp.where(i < nd, nd * qi + i, i - nd)

    def iteration(s_slot, k_new, q_slot, vt, p_slot):
        for (mp, c) in units:
            softmax(s_slot, mp, c)
            scores(k_new, q_slot, None, mp, c)
            pv(vt, p_slot, mp, c)

    def softmax_all(slot):
        for (mp, c) in units:
            softmax(slot, mp, c)

    def pv_all(kidx, slot):
        vt = vt_ref[kidx]
        for (mp, c) in units:
            pv(vt, slot, mp, c)

    def masked_scores(slot, d):
        kd = load_k(nd * qi + d)
        for (mp, c) in units:
            scores(kd, slot, d, mp, c)

    masked_scores(0, 0)
    softmax_all(0)
    masked_scores(1, 1)

    def pair(jj, carry):
        j = 2 * jj + 2
        iteration(1, load_k(j - nd), 0, vt_ref[key_of(j - 2)], 0)
        iteration(0, load_k(j + 1 - nd), 1, vt_ref[key_of(j - 1)], 1)
        return carry

    lax.fori_loop(0, qi, pair, 0)
    n = nd * qi + nd
    softmax_all(1)
    pv_all(key_of(n - 2), 0)
    pv_all(key_of(n - 1), 1)

    lam = _lambda_value(lq_ref, lk_ref, lam_init)
    gain = g_ref[...] * (1.0 - lam_init)
    for c in range(tq // Q_STRIP):
        ot = acc_ref[0, c] * (1.0 / l_ref[0, c]) - lam * (acc_ref[1, c] * (1.0 / l_ref[1, c]))
        ms = jnp.mean(ot * ot, axis=0, keepdims=True)
        ot = ot * lax.rsqrt(ms + RMS_EPS) * gain
        o_ref[c * Q_STRIP:(c + 1) * Q_STRIP, :] = ot.T.astype(o_ref.dtype)


def _flash_prompt(q, k, vt4, lam_q, lam_k, subln_g, *, batch, seq, tq, lam_init):
    tk = vt4.shape[-1]
    nkb = seq // tk
    ns = tq // Q_STRIP
    q3 = q.reshape(batch, seq, D_MODEL)
    k3 = k.reshape(batch, seq, D_MODEL)
    kern = functools.partial(_flash_prompt_kernel, tq=tq, tk=tk, lam_init=lam_init)
    small = lambda shape: pl.BlockSpec(shape, lambda b, h, i: (0, 0))
    out = pl.pallas_call(
        kern,
        grid=(batch, N_HEADS, seq // tq),
        in_specs=[small((2, HEAD_DIM)), small((2, HEAD_DIM)), small((V_DIM, 1)),
                  pl.BlockSpec((None, tq, V_DIM), lambda b, h, i: (b, i, h)),
                  pl.BlockSpec((None, seq, V_DIM), lambda b, h, i: (b, 0, h)),
                  pl.BlockSpec((None, nkb, V_DIM, tk), lambda b, h, i: (h, b, 0, 0))],
        out_specs=pl.BlockSpec((None, tq, V_DIM), lambda b, h, i: (b, i, h)),
        out_shape=jax.ShapeDtypeStruct((batch, seq, D_MODEL), BF16),
        scratch_shapes=[pltpu.VMEM((2, ns, 1, Q_STRIP), F32), pltpu.VMEM((2, ns, 1, Q_STRIP), F32),
                        pltpu.VMEM((2, ns, V_DIM, Q_STRIP), F32), pltpu.VMEM((2, 2, ns, tk, Q_STRIP), F32),
                        pltpu.VMEM((2, 2, ns, tk, Q_STRIP), BF16), pltpu.VMEM((2, 2, ns, 1, Q_STRIP), F32)],
        compiler_params=_params(("parallel", "parallel", "arbitrary")),
        name="flash_prompt",
    )(lam_q, lam_k, subln_g.reshape(V_DIM, 1), q3, k3, vt4)
    return out.reshape(batch * seq, D_MODEL)


def _attn_sample_kernel(lq_ref, lk_ref, g_ref, q_ref, kn_ref, vn_ref, kc_ref, vc_ref, o_ref, *, lam_init):
    q = q_ref[...]
    kc = kc_ref[...].astype(BF16)
    vc = vc_ref[...].astype(BF16)
    kn = kn_ref[...]
    vn = vn_ref[...]
    accs, ls = [], []
    for mp in range(2):
        sl = slice(mp * HEAD_DIM, (mp + 1) * HEAD_DIM)
        s_past = _nt_dot(q[:, sl], kc[:, sl])
        s_new = _nt_dot(q[:, sl], kn[:, sl])
        mx = jnp.maximum(jnp.max(s_past, axis=-1, keepdims=True), jnp.max(s_new, axis=-1, keepdims=True))
        p_past = jnp.exp2(s_past - mx)
        p_new = jnp.exp2(s_new - mx)
        ls.append(jnp.sum(p_past, axis=-1, keepdims=True) + jnp.sum(p_new, axis=-1, keepdims=True))
        accs.append(jnp.dot(p_past.astype(BF16), vc, preferred_element_type=F32)
                    + jnp.dot(p_new.astype(BF16), vn, preferred_element_type=F32))
    lam = _lambda_value(lq_ref, lk_ref, lam_init)
    _subln_store(o_ref, accs[0], ls[0], accs[1], ls[1], lam, g_ref, lam_init)


def _attn_sample(q, k_new, v_new, cache_k, cache_v, lam_q, lam_k, subln_g, *, n_streams, dec_seq, past_len,
                 lam_init):
    kern = functools.partial(_attn_sample_kernel, lam_init=lam_init)
    small = lambda shape: pl.BlockSpec(shape, lambda b, h: (0, 0))
    new = pl.BlockSpec((dec_seq, V_DIM), lambda b, h: (b, h))
    past = pl.BlockSpec((None, past_len, V_DIM), lambda b, h: (b, 0, h))
    return pl.pallas_call(
        kern,
        grid=(n_streams, N_HEADS),
        in_specs=[small((2, HEAD_DIM)), small((2, HEAD_DIM)), small((1, V_DIM)), new, new, new, past, past],
        out_specs=new,
        out_shape=jax.ShapeDtypeStruct((n_streams * dec_seq, D_MODEL), BF16),
        compiler_params=_params(("parallel", "parallel")),
        name="attn_sample",
    )(lam_q, lam_k, subln_g.reshape(1, V_DIM), q, k_new, v_new, cache_k, cache_v)


def _attn_branch_kernel(o_ref, w_ref, g_ref, out_ref):
    o = o_ref[...]
    for c in range(D_MODEL // V_DIM):
        sl = slice(c * V_DIM, (c + 1) * V_DIM)
        acc = jnp.dot(o, w_ref[:, sl], preferred_element_type=F32)
        out_ref[:, sl] = (g_ref[:, sl].astype(F32) * acc).astype(out_ref.dtype)


def _conv_branch_kernel(dw_ref, lng_ref, lnb_ref, w_ref, g_ref, ga_ref, out_ref, act_ref):
    x = dw_ref[...].astype(F32)
    mu = jnp.mean(x, axis=-1, keepdims=True)
    xc = x - mu
    var = jnp.mean(xc * xc, axis=-1, keepdims=True)
    y = xc * lax.rsqrt(var + LN_EPS) * lng_ref[...] + lnb_ref[...]
    act_ref[...] = (y * _sigmoid(y)).astype(BF16)
    for c in range(D_MODEL // V_DIM):
        sl = slice(c * V_DIM, (c + 1) * V_DIM)
        acc = jnp.dot(act_ref[...], w_ref[:, sl], preferred_element_type=F32)
        merged = ga_ref[:, sl].astype(F32) + g_ref[:, sl].astype(F32) * acc
        out_ref[:, sl] = merged.astype(out_ref.dtype)


def _router_combine(logits):
    lane = lax.broadcasted_iota(jnp.int32, logits.shape, 1)
    big = jnp.int32(ROUTER_LANES)
    gmask = lane < N_GROUPS
    gl = jnp.where(gmask, logits, -jnp.inf)
    gmax = jnp.max(gl, axis=-1, keepdims=True)
    grp = jnp.min(jnp.where(gl == gmax, lane, big), axis=-1, keepdims=True)
    denom = jnp.sum(jnp.where(gmask, jnp.exp(gl - gmax), 0.0), axis=-1, keepdims=True)
    p_grp = 1.0 / denom
    eidx = lane - N_GROUPS
    emask = (eidx >= 0) & (eidx < N_EXPERTS) & ((eidx // EXPERTS_PER_GROUP) == grp)
    ev = jnp.where(emask, logits, -jnp.inf)
    l1 = jnp.max(ev, axis=-1, keepdims=True)
    i1 = jnp.min(jnp.where(ev == l1, lane, big), axis=-1, keepdims=True)
    ev2 = jnp.where(lane == i1, -jnp.inf, ev)
    l2 = jnp.max(ev2, axis=-1, keepdims=True)
    i2 = jnp.min(jnp.where(ev2 == l2, lane, big), axis=-1, keepdims=True)
    e2 = jnp.exp(l2 - l1)
    w1 = p_grp / (1.0 + e2)
    w2 = p_grp * e2 / (1.0 + e2)
    comb = jnp.where(lane == i1, w1, 0.0) + jnp.where(lane == i2, w2, 0.0)
    return jnp.where(lane == 0, grp.astype(F32), comb)


def _out_proj_kernel(mg_ref, w_ref, x_ref, gffn_ref, wr_ref, br_ref, h_ref, hn_ref, comb_ref):
    mg = mg_ref[...]
    for c in range(D_MODEL // V_DIM):
        sl = slice(c * V_DIM, (c + 1) * V_DIM)
        h_ref[:, sl] = x_ref[:, sl] + jnp.dot(mg, w_ref[:, sl], preferred_element_type=F32)
    h = h_ref[...]
    ms = jnp.mean(h * h, axis=-1, keepdims=True)
    hn = h * lax.rsqrt(ms + RMS_EPS) * gffn_ref[...]
    hn_hi = hn.astype(BF16)
    hn_ref[:, :D_MODEL] = hn.astype(hn_ref.dtype)
    hn_lo = (hn - hn_hi.astype(F32)).astype(BF16)
    hw = jnp.dot(hn_hi, wr_ref[...], preferred_element_type=F32)
    lw = jnp.dot(hn_lo, wr_ref[:, :ROUTER_LANES], preferred_element_type=F32)
    logits = hw[:, :ROUTER_LANES] + hw[:, ROUTER_LANES:] + lw + br_ref[...]
    comb = _router_combine(logits)
    comb_ref[...] = comb
    if hn_ref.shape[1] > D_MODEL:
        hn_ref[:, D_MODEL:] = comb


def _token_spec(tm, width=D_MODEL, col=0):
    return pl.BlockSpec((tm, width), lambda i: (i, col))


def _full_spec(shape):
    return pl.BlockSpec(shape, lambda i: (0,) * len(shape))


def _attn_branch(o, w_attn_bf, gates, *, tm):
    m = o.shape[0]
    return pl.pallas_call(
        _attn_branch_kernel,
        grid=(m // tm,),
        in_specs=[_token_spec(tm), _full_spec((ATTN_W, D_MODEL)), _token_spec(tm, D_MODEL, 0)],
        out_specs=_token_spec(tm),
        out_shape=jax.ShapeDtypeStruct((m, D_MODEL), BF16),
        compiler_params=_params(("parallel",)),
        name="attn_branch",
    )(o, w_attn_bf, gates)


def _conv_branch(dw, ln_g, ln_b, w_conv_bf, gates, ga, *, tm):
    m = dw.shape[0]
    return pl.pallas_call(
        _conv_branch_kernel,
        grid=(m // tm,),
        in_specs=[_token_spec(tm), _full_spec((1, C_CONV)), _full_spec((1, C_CONV)),
                  _full_spec((C_CONV, D_MODEL)), _token_spec(tm, D_MODEL, 1), _token_spec(tm)],
        out_specs=_token_spec(tm),
        out_shape=jax.ShapeDtypeStruct((m, D_MODEL), BF16),
        scratch_shapes=[pltpu.VMEM((tm, C_CONV), BF16)],
        compiler_params=_params(("parallel",)),
        name="conv_branch",
    )(dw, ln_g.reshape(1, C_CONV), ln_b.reshape(1, C_CONV), w_conv_bf, gates, ga)


def _out_proj(merged, w_out_bf, x, g_ffn, w_router, b_router, *, tm, with_router_lanes):
    m = merged.shape[0]
    hn_width = D_MODEL + ROUTER_LANES if with_router_lanes else D_MODEL
    hn_dtype = F32 if with_router_lanes else BF16
    return pl.pallas_call(
        _out_proj_kernel,
        grid=(m // tm,),
        in_specs=[_token_spec(tm), _full_spec((D_MODEL, D_MODEL)), _token_spec(tm), _full_spec((1, D_MODEL)),
                  _full_spec((D_MODEL, 2 * ROUTER_LANES)), _full_spec((1, ROUTER_LANES))],
        out_specs=[_token_spec(tm), _token_spec(tm, hn_width), _token_spec(tm, ROUTER_LANES)],
        out_shape=[jax.ShapeDtypeStruct((m, D_MODEL), F32), jax.ShapeDtypeStruct((m, hn_width), hn_dtype),
                   jax.ShapeDtypeStruct((m, ROUTER_LANES), F32)],
        compiler_params=_params(("parallel",)),
        name="out_proj_router",
    )(merged, w_out_bf, x, g_ffn.reshape(1, D_MODEL), w_router, b_router)


def _moe_dense_kernel(hn_ref, h_ref, comb_ref, wg_ref, wu_ref, wd_ref, gfin_ref, y_ref, acc_ref):
    e = pl.program_id(1)

    @pl.when(e == 0)
    def _():
        acc_ref[...] = h_ref[...]

    x = hn_ref[...]
    gate = jnp.dot(x, wg_ref[...], preferred_element_type=F32)
    up = jnp.dot(x, wu_ref[...], preferred_element_type=F32)
    comb = comb_ref[...]
    lane = lax.broadcasted_iota(jnp.int32, comb.shape, 1)
    c = jnp.sum(jnp.where(lane == e + N_GROUPS, comb, 0.0), axis=-1, keepdims=True)
    hid = (gate * _sigmoid(gate) * up * c).astype(BF16)
    acc_ref[...] += jnp.dot(hid, wd_ref[...], preferred_element_type=F32)

    @pl.when(e == N_EXPERTS - 1)
    def _():
        y = acc_ref[...]
        ms = jnp.mean(y * y, axis=-1, keepdims=True)
        y_ref[...] = y * lax.rsqrt(ms + RMS_EPS) * gfin_ref[...]


def _moe_dense(hn, h, comb, wg_bf, wu_bf, wd_bf, g_final, *, tm):
    m = hn.shape[0]
    return pl.pallas_call(
        _moe_dense_kernel,
        grid=(m // tm, N_EXPERTS),
        in_specs=[pl.BlockSpec((tm, D_MODEL), lambda i, e: (i, 0)),
                  pl.BlockSpec((tm, D_MODEL), lambda i, e: (i, 0)),
                  pl.BlockSpec((tm, ROUTER_LANES), lambda i, e: (i, 0)),
                  pl.BlockSpec((None, D_MODEL, D_EXPERT), lambda i, e: (e, 0, 0)),
                  pl.BlockSpec((None, D_MODEL, D_EXPERT), lambda i, e: (e, 0, 0)),
                  pl.BlockSpec((None, D_EXPERT, D_MODEL), lambda i, e: (e, 0, 0)),
                  pl.BlockSpec((1, D_MODEL), lambda i, e: (0, 0))],
        out_specs=pl.BlockSpec((tm, D_MODEL), lambda i, e: (i, 0)),
        out_shape=jax.ShapeDtypeStruct((m, D_MODEL), F32),
        scratch_shapes=[pltpu.VMEM((tm, D_MODEL), F32)],
        compiler_params=_params(("parallel", "arbitrary")),
        name="moe_dense",
    )(hn, h, comb, wg_bf, wu_bf, wd_bf, g_final.reshape(1, D_MODEL))


ATTN_W = N_HEADS * V_DIM


def _moe_group_kernel(tg_ref, src_ref, hx_hbm, wg_ref, wu_ref, wd_ref, y_ref, xbuf, xb_ref, comb_scr, sems, *, rows):
    i = pl.program_id(0)
    e = pl.program_id(1)
    n_tiles = pl.num_programs(0)

    def issue(tile, slot):
        base = tile * rows

        def body(r, carry):
            tok = src_ref[base + r]
            pltpu.make_async_copy(hx_hbm.at[pl.ds(tok, 1)], xbuf.at[slot, pl.ds(r, 1)], sems.at[slot]).start()
            return carry

        lax.fori_loop(0, rows, body, 0, unroll=8)

    @pl.when(e == 0)
    def _():
        slot = i % 2

        @pl.when(i == 0)
        def _():
            issue(0, 0)

        @pl.when(i + 1 < n_tiles)
        def _():
            issue(i + 1, 1 - slot)

        pltpu.make_async_copy(hx_hbm.at[pl.ds(0, rows)], xbuf.at[slot], sems.at[slot]).wait()
        xb_ref[...] = xbuf[slot, :, :D_MODEL].astype(BF16)
        comb_scr[...] = xbuf[slot, :, D_MODEL:]
        y_ref[...] = jnp.zeros(y_ref.shape, F32)

    x = xb_ref[...]
    gate = jnp.dot(x, wg_ref[...], preferred_element_type=F32)
    up = jnp.dot(x, wu_ref[...], preferred_element_type=F32)
    comb = comb_scr[...]
    lane = lax.broadcasted_iota(jnp.int32, comb.shape, 1)
    expert_lane = N_GROUPS + tg_ref[i] * EXPERTS_PER_GROUP + e
    c = jnp.sum(jnp.where(lane == expert_lane, comb, 0.0), axis=-1, keepdims=True)
    hid = (gate * _sigmoid(gate) * up * c).astype(BF16)
    y_ref[...] += jnp.dot(hid, wd_ref[...], preferred_element_type=F32)


def _moe_group(tile_group, src, hx, wg_bf, wu_bf, wd_bf, *, rows):
    p = src.shape[0]
    w_idx = lambda i, e, tg, sr: (tg[i] * EXPERTS_PER_GROUP + e, 0, 0)
    grid_spec = pltpu.PrefetchScalarGridSpec(
        num_scalar_prefetch=2,
        grid=(p // rows, EXPERTS_PER_GROUP),
        in_specs=[pl.BlockSpec(memory_space=pl.ANY),
                  pl.BlockSpec((None, D_MODEL, D_EXPERT), w_idx),
                  pl.BlockSpec((None, D_MODEL, D_EXPERT), w_idx),
                  pl.BlockSpec((None, D_EXPERT, D_MODEL), w_idx)],
        out_specs=pl.BlockSpec((rows, D_MODEL), lambda i, e, tg, sr: (i, 0)),
        scratch_shapes=[pltpu.VMEM((2, rows, D_MODEL + ROUTER_LANES), F32),
                        pltpu.VMEM((rows, D_MODEL), BF16),
                        pltpu.VMEM((rows, ROUTER_LANES), F32),
                        pltpu.SemaphoreType.DMA((2,))],
    )
    return pl.pallas_call(
        functools.partial(_moe_group_kernel, rows=rows),
        grid_spec=grid_spec,
        out_shape=jax.ShapeDtypeStruct((p, D_MODEL), F32),
        compiler_params=_params(("arbitrary", "arbitrary")),
        name="moe_group",
    )(tile_group, src, hx, wg_bf, wu_bf, wd_bf)


def _moe_finish_kernel(dest_ref, ys_hbm, h_ref, g_ref, y_ref, buf_ref, sem, *, rows):
    base = pl.program_id(0) * rows

    def issue(r, carry):
        p = dest_ref[base + r]
        pltpu.make_async_copy(ys_hbm.at[pl.ds(p, 1)], buf_ref.at[pl.ds(r, 1)], sem).start()
        return carry

    lax.fori_loop(0, rows, issue, 0, unroll=8)
    pltpu.make_async_copy(ys_hbm.at[pl.ds(0, rows)], buf_ref, sem).wait()
    y = h_ref[...] + buf_ref[...]
    ms = jnp.mean(y * y, axis=-1, keepdims=True)
    y_ref[...] = y * lax.rsqrt(ms + RMS_EPS) * g_ref[...]


def _moe_finish(dest, y_sorted, h, g_final, *, rows):
    m = h.shape[0]
    grid_spec = pltpu.PrefetchScalarGridSpec(
        num_scalar_prefetch=1,
        grid=(m // rows,),
        in_specs=[pl.BlockSpec(memory_space=pl.ANY),
                  pl.BlockSpec((rows, D_MODEL), lambda i, d: (i, 0)),
                  pl.BlockSpec((1, D_MODEL), lambda i, d: (0, 0))],
        out_specs=pl.BlockSpec((rows, D_MODEL), lambda i, d: (i, 0)),
        scratch_shapes=[pltpu.VMEM((rows, D_MODEL), F32), pltpu.SemaphoreType.DMA],
    )
    return pl.pallas_call(
        functools.partial(_moe_finish_kernel, rows=rows),
        grid_spec=grid_spec,
        out_shape=jax.ShapeDtypeStruct((m, D_MODEL), F32),
        compiler_params=_params(("arbitrary",)),
        name="moe_finish",
    )(dest, y_sorted, h, g_final.reshape(1, D_MODEL))


def _moe_routed(hx, h, comb, wg_bf, wu_bf, wd_bf, g_final, *, rows):
    m = hx.shape[0]
    grp = comb[:, 0].astype(jnp.int32)
    onehot = (grp[:, None] == jnp.arange(N_GROUPS, dtype=jnp.int32)[None, :]).astype(jnp.int32)
    rank = jnp.cumsum(onehot, axis=0) - onehot
    counts = jnp.sum(onehot, axis=0)
    padded = ((counts + rows - 1) // rows) * rows
    seg_end = jnp.cumsum(padded)
    seg_start = seg_end - padded
    dest = jnp.sum(onehot * (seg_start[None, :] + rank), axis=1).astype(jnp.int32)
    p_rows = m + N_GROUPS * rows
    src = jnp.zeros((p_rows,), jnp.int32).at[dest].set(jnp.arange(m, dtype=jnp.int32))
    tile_start = jnp.arange(p_rows // rows, dtype=jnp.int32) * rows
    tile_group = jnp.minimum(jnp.sum((tile_start[:, None] >= seg_end[None, :]).astype(jnp.int32), axis=1),
                             N_GROUPS - 1).astype(jnp.int32)
    y_sorted = _moe_group(tile_group, src, hx, wg_bf, wu_bf, wd_bf, rows=rows)
    return _moe_finish(dest, y_sorted, h, g_final, rows=rows)


def _rope_tables(pos):
    half = HEAD_DIM // 2
    inv_freq = ROPE_THETA ** (-jnp.arange(half, dtype=F32) / half)
    ang = pos.astype(F32)[:, None] * inv_freq[None, :]
    cos, sin = jnp.cos(ang), jnp.sin(ang)
    return jnp.concatenate([cos, cos], axis=-1), jnp.concatenate([-sin, sin], axis=-1)


def _layer_tokens(x, pos_tables, init_state, attn_fn, wts, *, tm, n_seq, seq_len, tiles_per_seq, v_kind, routed):
    cos_t, sin_t = pos_tables
    xn = _rmsnorm(x, wts["norm_mix_g"], tm)
    (q,) = _proj_rope(xn, wts["w_in"], 0, cos_t, sin_t, tm=tm, rope=True, scale=SCALE * LOG2_E,
                      outs=(("tokens", BF16),), name="proj_q")
    k_f32, k_bf = _proj_rope(xn, wts["w_in"], 1, cos_t, sin_t, tm=tm, rope=True, scale=1.0,
                             outs=(("tokens", F32), ("tokens", BF16)), name="proj_k")
    v_f32, v_bf = _proj_rope(xn, wts["w_in"], 2, cos_t, sin_t, tm=tm, rope=False, scale=1.0,
                             outs=(("tokens", F32), (v_kind, BF16)), name="proj_v")
    dw, new_conv = _proj_glu_conv(xn, wts["w_glu"], init_state, wts["w_dw"], wts["b_dw"],
                                  n_seq=n_seq, seq_len=seq_len, tiles_per_seq=tiles_per_seq)
    gates = _proj_gate(xn, wts["w_in"], 5, wts["b_gate"], tm=tm)
    o = attn_fn(q, k_bf, v_bf)
    ga = _attn_branch(o, wts["w_attn"], gates, tm=tm)
    merged = _conv_branch(dw, wts["conv_ln_g"], wts["conv_ln_b"], wts["w_conv"], gates, ga, tm=tm)
    h, hn, comb = _out_proj(merged, wts["w_out"], x, wts["norm_ffn_g"], wts["w_router"], wts["b_router"], tm=tm,
                            with_router_lanes=routed)
    moe = functools.partial(_moe_routed, rows=MOE_ROWS) if routed else functools.partial(_moe_dense, tm=tm)
    y = moe(hn, h, comb, wts["w_eg"], wts["w_eu"], wts["w_ed"], wts["norm_final_g"])
    return y, k_f32, v_f32, new_conv


def kernel(x_prompt, x_sample, cache_k, cache_v, state_conv, norm_mix_g, w_in, b_gate, lambda_q, lambda_k, subln_g, w_attn_branch, w_dw, b_dw, conv_ln_g, conv_ln_b, w_conv_branch, w_out, norm_ffn_g, w_router_group, b_router_group, w_router_expert, b_router_expert, w_exp_gate, w_exp_up, w_exp_down, norm_final_g):
    batch, seq, _ = x_prompt.shape
    n_streams, dec_seq, _ = x_sample.shape
    depth, _, past_len = cache_k.shape[:3]
    assert depth == 1
    assert past_len % CHUNK == 0 and dec_seq <= CHUNK
    l = 0
    lam_init = 0.8 - 0.6 * math.exp(-0.3 * l)

    pad_r = ROUTER_LANES - N_GROUPS - N_EXPERTS
    w_router = jnp.concatenate([w_router_group[l], w_router_expert[l], jnp.zeros((D_MODEL, pad_r), F32)], axis=1)
    b_router = jnp.concatenate([b_router_group[l], b_router_expert[l], jnp.zeros((pad_r,), F32)]).reshape(1, -1)
    w_router_hi = w_router.astype(BF16)
    w_router_lo = (w_router - w_router_hi.astype(F32)).astype(BF16)
    w_router = jnp.concatenate([w_router_hi, w_router_lo], axis=1)
    w_in_bf = w_in[l].astype(BF16)
    n_chunks = C_CONV // LANES
    glu_cols = N_HEADS * (4 * HEAD_DIM + V_DIM)
    w_val = w_in_bf[:, glu_cols:glu_cols + C_CONV].reshape(D_MODEL, n_chunks, 1, LANES)
    w_gat = w_in_bf[:, glu_cols + C_CONV:glu_cols + 2 * C_CONV].reshape(D_MODEL, n_chunks, 1, LANES)
    w_glu = jnp.concatenate([w_val, w_gat], axis=2).reshape(D_MODEL, n_chunks // GLU_CHUNKS, GLU_CHUNKS * 2 * LANES)
    w_glu = w_glu.transpose(1, 0, 2)
    wts = dict(
        norm_mix_g=norm_mix_g[l], w_in=w_in_bf, w_glu=w_glu, b_gate=b_gate[l],
        w_attn=w_attn_branch[l].astype(BF16), w_dw=w_dw[l], b_dw=b_dw[l],
        conv_ln_g=conv_ln_g[l], conv_ln_b=conv_ln_b[l], w_conv=w_conv_branch[l].astype(BF16),
        w_out=w_out[l].astype(BF16), norm_ffn_g=norm_ffn_g[l], w_router=w_router, b_router=b_router,
        w_eg=w_exp_gate[l].astype(BF16), w_eu=w_exp_up[l].astype(BF16), w_ed=w_exp_down[l].astype(BF16),
        norm_final_g=norm_final_g)
    lam_q, lam_k, sub_g = lambda_q[l], lambda_k[l], subln_g[l]

    tm = 512
    tabs_p = _rope_tables(jnp.arange(seq, dtype=jnp.int32))
    init_p = jnp.zeros((batch, HIST_ROWS, C_CONV), F32)
    attn_p = functools.partial(_flash_prompt, lam_q=lam_q, lam_k=lam_k, subln_g=sub_g,
                               batch=batch, seq=seq, tq=2 * tm, lam_init=lam_init)
    y_p, k_p, v_p, conv_p = _layer_tokens(
        x_prompt.reshape(batch * seq, D_MODEL), tabs_p, init_p, attn_p, wts,
        tm=tm, n_seq=1, seq_len=tm, tiles_per_seq=seq // tm, v_kind="head_transposed", routed=True)

    cos_s, sin_s = _rope_tables(past_len + jnp.arange(dec_seq, dtype=jnp.int32))
    tabs_s = (jnp.tile(cos_s, (n_streams, 1)), jnp.tile(sin_s, (n_streams, 1)))
    init_s = jnp.pad(state_conv[l], ((0, 0), (HIST_OFF, 0), (0, 0)))
    ck = cache_k[l].reshape(n_streams, past_len, D_MODEL)
    cv = cache_v[l].reshape(n_streams, past_len, D_MODEL)
    attn_s = functools.partial(_attn_sample, cache_k=ck, cache_v=cv, lam_q=lam_q, lam_k=lam_k, subln_g=sub_g,
                               n_streams=n_streams, dec_seq=dec_seq, past_len=past_len, lam_init=lam_init)
    m_s = n_streams * dec_seq
    y_s, k_s, v_s, conv_s = _layer_tokens(
        x_sample.reshape(m_s, D_MODEL), tabs_s, init_s, attn_s, wts,
        tm=m_s, n_seq=n_streams, seq_len=dec_seq, tiles_per_seq=1, v_kind="tokens", routed=False)

    return (y_p.reshape(batch, seq, D_MODEL),
            y_s.reshape(n_streams, dec_seq, D_MODEL),
            k_p.reshape(1, batch, seq, N_HEADS, 2, HEAD_DIM),
            v_p.reshape(1, batch, seq, N_HEADS, V_DIM),
            conv_p.reshape(1, batch, CONV_K - 1, C_CONV),
            k_s.reshape(1, n_streams, dec_seq, N_HEADS, 2, HEAD_DIM),
            v_s.reshape(1, n_streams, dec_seq, N_HEADS, V_DIM),
            conv_s.reshape(1, n_streams, CONV_K - 1, C_CONV))
```

```python
import functools
import math

import jax
import jax.numpy as jnp
from jax import lax
from jax.experimental import pallas as pl
from jax.experimental.pallas import tpu as pltpu

F32 = jnp.float32
BF16 = jnp.bfloat16

D_MODEL = 2048
CHUNK = 64
HEAD_DIM = 128
N_HEADS = D_MODEL // (2 * HEAD_DIM)
V_DIM = 2 * HEAD_DIM
C_CONV = D_MODEL
CONV_K = 31
N_GROUPS = 4
EXPERTS_PER_GROUP = 4
N_EXPERTS = N_GROUPS * EXPERTS_PER_GROUP
D_EXPERT = D_MODEL // 4
ROPE_THETA = 10000.0
RMS_EPS = 1e-6
LN_EPS = 1e-5
NEG_INF = -1e30
SCALE = HEAD_DIM ** -0.5
LOG2_E = math.log2(math.e)
Q_STRIP = 256

LANES = 128
SUBLANES = 8
GLU_CHUNKS = 4
MOE_ROWS = 512
HIST_ROWS = 32
HIST_OFF = HIST_ROWS - (CONV_K - 1)
ROUTER_LANES = 128
VMEM_LIMIT = 56 * 1024 * 1024


def _params(sem):
    return pltpu.CompilerParams(dimension_semantics=sem, vmem_limit_bytes=VMEM_LIMIT)


def _sigmoid(x):
    return 1.0 / (1.0 + jnp.exp(-x))


def _rmsnorm_kernel(x_ref, g_ref, o_ref):
    x = x_ref[...]
    ms = jnp.mean(x * x, axis=-1, keepdims=True)
    o_ref[...] = (x * lax.rsqrt(ms + RMS_EPS) * g_ref[...]).astype(o_ref.dtype)


def _rmsnorm(x, g, tm):
    m = x.shape[0]
    return pl.pallas_call(
        _rmsnorm_kernel,
        grid=(m // tm,),
        in_specs=[pl.BlockSpec((tm, D_MODEL), lambda i: (i, 0)),
                  pl.BlockSpec((1, D_MODEL), lambda i: (0, 0))],
        out_specs=pl.BlockSpec((tm, D_MODEL), lambda i: (i, 0)),
        out_shape=jax.ShapeDtypeStruct((m, D_MODEL), BF16),
        compiler_params=_params(("parallel",)),
        name="rmsnorm_in",
    )(x, g.reshape(1, D_MODEL))


def _proj_rope_kernel(x_ref, w_ref, cos_ref, sin_ref, *out_refs, scale, rope, tn, out_kinds):
    x = x_ref[...]
    for c in range(tn // V_DIM):
        acc = jnp.dot(x, w_ref[:, c * V_DIM:(c + 1) * V_DIM], preferred_element_type=F32)
        if rope:
            cs = cos_ref[...]
            sn = sin_ref[...]
            halves = []
            for m in range(2):
                a = acc[:, m * HEAD_DIM:(m + 1) * HEAD_DIM]
                halves.append(a * cs + pltpu.roll(a, HEAD_DIM // 2, axis=1) * sn)
            acc = jnp.concatenate(halves, axis=1)
        if scale != 1.0:
            acc = acc * scale
        for o_ref, kind in zip(out_refs, out_kinds):
            if kind == "head_transposed":
                o_ref[c] = acc.T.astype(o_ref.dtype)
            else:
                o_ref[:, c * V_DIM:(c + 1) * V_DIM] = acc.astype(o_ref.dtype)


def _proj_rope(xn, w_bf, col_block, cos_t, sin_t, *, tm, rope, scale, outs, name):
    m = xn.shape[0]
    tn = D_MODEL
    t_tiles = cos_t.shape[0] // tm
    kinds = tuple(k for k, _ in outs)
    kern = functools.partial(_proj_rope_kernel, scale=scale, rope=rope, tn=tn, out_kinds=kinds)
    out_specs, out_shape = [], []
    for kind, dt in outs:
        if kind == "head_transposed":
            out_specs.append(pl.BlockSpec((N_HEADS, None, V_DIM, tm), lambda i: (0, i, 0, 0)))
            out_shape.append(jax.ShapeDtypeStruct((N_HEADS, m // tm, V_DIM, tm), dt))
        else:
            out_specs.append(pl.BlockSpec((tm, tn), lambda i: (i, 0)))
            out_shape.append(jax.ShapeDtypeStruct((m, tn), dt))
    return pl.pallas_call(
        kern,
        grid=(m // tm,),
        in_specs=[pl.BlockSpec((tm, D_MODEL), lambda i: (i, 0)),
                  pl.BlockSpec((D_MODEL, tn), lambda i: (0, col_block)),
                  pl.BlockSpec((tm, HEAD_DIM), lambda i: (i % t_tiles, 0)),
                  pl.BlockSpec((tm, HEAD_DIM), lambda i: (i % t_tiles, 0))],
        out_specs=out_specs,
        out_shape=out_shape,
        compiler_params=_params(("parallel",)),
        name=name,
    )(xn, w_bf, cos_t, sin_t)


def _proj_gate_kernel(x_ref, w_ref, b_ref, o_ref, *, tn):
    x = x_ref[...]
    for c in range(tn // V_DIM):
        sl = slice(c * V_DIM, (c + 1) * V_DIM)
        acc = jnp.dot(x, w_ref[:, sl], preferred_element_type=F32) + b_ref[:, sl]
        o_ref[:, sl] = _sigmoid(acc).astype(o_ref.dtype)


def _proj_gate(xn, w_bf, col_block, b_gate, *, tm):
    m = xn.shape[0]
    tn = D_MODEL
    kern = functools.partial(_proj_gate_kernel, tn=tn)
    return pl.pallas_call(
        kern,
        grid=(2, m // tm),
        in_specs=[pl.BlockSpec((tm, D_MODEL), lambda j, i: (i, 0)),
                  pl.BlockSpec((D_MODEL, tn), lambda j, i: (0, col_block + j)),
                  pl.BlockSpec((1, tn), lambda j, i: (0, j))],
        out_specs=pl.BlockSpec((tm, tn), lambda j, i: (i, j)),
        out_shape=jax.ShapeDtypeStruct((m, 2 * D_MODEL), BF16),
        compiler_params=_params(("parallel", "parallel")),
        name="proj_gate",
    )(xn, w_bf, b_gate.reshape(1, 2 * D_MODEL))


def _odd_stride(seq_len):
    s = -(-seq_len // SUBLANES)
    return s if s % 2 == 1 else s + 1


def _proj_glu_conv_kernel(x_ref, w_ref, init_ref, wdw_ref, bdw_ref, dw_ref, state_ref,
                          pad_ref, out_scr, carry_ref, *, n_seq, seq_len, tiles_per_seq):
    tile = pl.program_id(1)
    stride = _odd_stride(seq_len)
    pad_rows = pad_ref.shape[2]
    group = next(g for g in (13, 9, 5, 3, 1) if stride % g == 0)
    x = x_ref[...]
    for q in range(GLU_CHUNKS):
        ls = slice(q * LANES, (q + 1) * LANES)
        ab = jnp.dot(x, w_ref[:, q * 2 * LANES:(q + 1) * 2 * LANES], preferred_element_type=F32)
        glu = ab[:, :LANES] * _sigmoid(ab[:, LANES:])
        for s in range(n_seq):
            if tiles_per_seq == 1:
                pad_ref[q, s, 0:HIST_ROWS, :] = init_ref[s, :, ls]
            else:
                first = (tile % tiles_per_seq) == 0
                pad_ref[q, s, 0:HIST_ROWS, :] = jnp.where(first, init_ref[s, :, ls], carry_ref[q])
            pad_ref[q, s, HIST_ROWS:HIST_ROWS + seq_len, :] = glu[s * seq_len:(s + 1) * seq_len, :]
            pad_ref[q, s, HIST_ROWS + seq_len:pad_rows, :] = jnp.zeros((pad_rows - HIST_ROWS - seq_len, LANES), F32)
            if tiles_per_seq != 1:
                carry_ref[q] = pad_ref[q, s, seq_len:seq_len + HIST_ROWS, :]
            state_ref[s, :, ls] = pad_ref[q, s, seq_len + HIST_OFF:seq_len + HIST_ROWS, :]

        bias = jnp.broadcast_to(bdw_ref[:, ls], (SUBLANES, LANES))
        for s in range(n_seq):
            for g0 in range(0, stride, group):
                accs = [bias] * group
                for j in range(CONV_K):
                    w = jnp.broadcast_to(wdw_ref[j:j + 1, ls], (SUBLANES, LANES))
                    for k in range(group):
                        rows = pl.ds(g0 + k + HIST_OFF + j, SUBLANES, stride=stride)
                        accs[k] = accs[k] + w * pad_ref[q, s, rows, :]
                for k in range(group):
                    out_scr[q, s, pl.ds(g0 + k, SUBLANES, stride=stride), :] = accs[k]
            dw_ref[s * seq_len:(s + 1) * seq_len, ls] = out_scr[q, s, 0:seq_len, :].astype(dw_ref.dtype)


def _proj_glu_conv(xn, w_glu, init_state, w_dw, b_dw, *, n_seq, seq_len, tiles_per_seq):
    m = xn.shape[0]
    tm = n_seq * seq_len
    uw = GLU_CHUNKS * LANES
    n_sequences = init_state.shape[0]
    stride = _odd_stride(seq_len)
    pad_rows = -(-(HIST_ROWS + SUBLANES * stride + SUBLANES) // SUBLANES) * SUBLANES
    kern = functools.partial(_proj_glu_conv_kernel, n_seq=n_seq, seq_len=seq_len, tiles_per_seq=tiles_per_seq)
    return pl.pallas_call(
        kern,
        grid=(C_CONV // uw, m // tm),
        in_specs=[pl.BlockSpec((tm, D_MODEL), lambda c, i: (i, 0)),
                  pl.BlockSpec((None, D_MODEL, 2 * uw), lambda c, i: (c, 0, 0)),
                  pl.BlockSpec((n_seq, HIST_ROWS, uw), lambda c, i: (i // tiles_per_seq, 0, c)),
                  pl.BlockSpec((CONV_K, uw), lambda c, i: (0, c)),
                  pl.BlockSpec((1, uw), lambda c, i: (0, c))],
        out_specs=[pl.BlockSpec((tm, uw), lambda c, i: (i, c)),
                   pl.BlockSpec((n_seq, CONV_K - 1, uw), lambda c, i: (i // tiles_per_seq, 0, c))],
        out_shape=[jax.ShapeDtypeStruct((m, C_CONV), BF16),
                   jax.ShapeDtypeStruct((n_sequences, CONV_K - 1, C_CONV), F32)],
        scratch_shapes=[pltpu.VMEM((GLU_CHUNKS, n_seq, pad_rows, LANES), F32),
                        pltpu.VMEM((GLU_CHUNKS, n_seq, SUBLANES * stride, LANES), F32),
                        pltpu.VMEM((GLU_CHUNKS, HIST_ROWS, LANES), F32)],
        compiler_params=_params(("parallel", "arbitrary")),
        name="proj_glu_conv",
    )(xn, w_glu, init_state, w_dw, b_dw.reshape(1, C_CONV))


def _lambda_value(lq_ref, lk_ref, lam_init):
    prod = lq_ref[...] * lk_ref[...]
    d = jnp.sum(prod, axis=1, keepdims=True)
    e = jnp.exp(d)
    return e[0:1, :] - e[1:2, :] + lam_init


def _subln_store(o_ref, acc0, l0, acc1, l1, lam, g_ref, lam_init):
    o = acc0 / l0 - lam * (acc1 / l1)
    ms = jnp.mean(o * o, axis=-1, keepdims=True)
    o = o * lax.rsqrt(ms + RMS_EPS) * g_ref[...] * (1.0 - lam_init)
    o_ref[...] = o.astype(o_ref.dtype)


def _nt_dot(a, b):
    return lax.dot_general(a, b, (((1,), (1,)), ((), ())), preferred_element_type=F32)


def _flash_prompt_kernel(lq_ref, lk_ref, g_ref, q_ref, k_ref, vt_ref, o_ref,
                         m_ref, l_ref, acc_ref, st_ref, p_ref, alpha_ref, *, tq, tk, lam_init):
    nd = tq // tk
    assert nd == 2
    qi = pl.program_id(2)
    q = q_ref[...]
    m_ref[...] = jnp.full(m_ref.shape, NEG_INF, F32)
    l_ref[...] = jnp.zeros(l_ref.shape, F32)
    acc_ref[...] = jnp.zeros(acc_ref.shape, F32)
    units = [(mp, c) for c in range(tq // Q_STRIP) for mp in range(2)]

    def scores(k, slot, diag, mp, c):
        sl = slice(mp * HEAD_DIM, (mp + 1) * HEAD_DIM)
        cs = slice(c * Q_STRIP, (c + 1) * Q_STRIP)
        st = _nt_dot(k[:, sl], q[cs, sl])
        if diag is not None:
            key_chunk = (lax.broadcasted_iota(jnp.int32, (tk, Q_STRIP), 0) + diag * tk) // CHUNK
            qry_chunk = (lax.broadcasted_iota(jnp.int32, (tk, Q_STRIP), 1) + c * Q_STRIP) // CHUNK
            st = jnp.where(key_chunk <= qry_chunk, st, NEG_INF)
        st_ref[slot, mp, c] = st

    def softmax(slot, mp, c):
        st = st_ref[slot, mp, c]
        m_old = m_ref[mp, c]
        m_new = jnp.maximum(m_old, jnp.max(st, axis=0, keepdims=True))
        alpha = jnp.exp2(m_old - m_new)
        p = jnp.exp2(st - m_new)
        l_ref[mp, c] = l_ref[mp, c] * alpha + jnp.sum(p, axis=0, keepdims=True)
        m_ref[mp, c] = m_new
        p_ref[slot, mp, c] = p.astype(BF16)
        alpha_ref[slot, mp, c] = alpha

    def pv(vt, slot, mp, c):
        acc_ref[mp, c] = (acc_ref[mp, c] * alpha_ref[slot, mp, c]
                          + jnp.dot(vt, p_ref[slot, mp, c], preferred_element_type=F32))

    def load_k(kidx):
        return k_ref[pl.ds(pl.multiple_of(kidx * tk, tk), tk), :]

    def key_of(i):
        return jnp.where(i < nd, nd * qi + i, i - nd)

    def iteration(s_slot, k_new, q_slot, vt, p_slot):
        for (mp, c) in units:
            softmax(s_slot, mp, c)
            scores(k_new, q_slot, None, mp, c)
            pv(vt, p_slot, mp, c)

    def softmax_all(slot):
        for (mp, c) in units:
            softmax(slot, mp, c)

    def pv_all(kidx, slot):
        vt = vt_ref[kidx]
        for (mp, c) in units:
            pv(vt, slot, mp, c)

    def masked_scores(slot, d):
        kd = load_k(nd * qi + d)
        for (mp, c) in units:
            scores(kd, slot, d, mp, c)

    masked_scores(0, 0)
    softmax_all(0)
    masked_scores(1, 1)

    def pair(jj, carry):
        j = 2 * jj + 2
        iteration(1, load_k(j - nd), 0, vt_ref[key_of(j - 2)], 0)
        iteration(0, load_k(j + 1 - nd), 1, vt_ref[key_of(j - 1)], 1)
        return carry

    lax.fori_loop(0, qi, pair, 0)
    n = nd * qi + nd
    softmax_all(1)
    pv_all(key_of(n - 2), 0)
    pv_all(key_of(n - 1), 1)

    lam = _lambda_value(lq_ref, lk_ref, lam_init)
    gain = g_ref[...] * (1.0 - lam_init)
    for c in range(tq // Q_STRIP):
        ot = acc_ref[0, c] * (1.0 / l_ref[0, c]) - lam * (acc_ref[1, c] * (1.0 / l_ref[1, c]))
        ms = jnp.mean(ot * ot, axis=0, keepdims=True)
        ot = ot * lax.rsqrt(ms + RMS_EPS) * gain
        o_ref[c * Q_STRIP:(c + 1) * Q_STRIP, :] = ot.T.astype(o_ref.dtype)


def _flash_prompt(q, k, vt4, lam_q, lam_k, subln_g, *, batch, seq, tq, lam_init):
    tk = vt4.shape[-1]
    nkb = seq // tk
    ns = tq // Q_STRIP
    q3 = q.reshape(batch, seq, D_MODEL)
    k3 = k.reshape(batch, seq, D_MODEL)
    kern = functools.partial(_flash_prompt_kernel, tq=tq, tk=tk, lam_init=lam_init)
    small = lambda shape: pl.BlockSpec(shape, lambda b, h, i: (0, 0))
    out = pl.pallas_call(
        kern,
        grid=(batch, N_HEADS, seq // tq),
        in_specs=[small((2, HEAD_DIM)), small((2, HEAD_DIM)), small((V_DIM, 1)),
                  pl.BlockSpec((None, tq, V_DIM), lambda b, h, i: (b, i, h)),
                  pl.BlockSpec((None, seq, V_DIM), lambda b, h, i: (b, 0, h)),
                  pl.BlockSpec((None, nkb, V_DIM, tk), lambda b, h, i: (h, b, 0, 0))],
        out_specs=pl.BlockSpec((None, tq, V_DIM), lambda b, h, i: (b, i, h)),
        out_shape=jax.ShapeDtypeStruct((batch, seq, D_MODEL), BF16),
        scratch_shapes=[pltpu.VMEM((2, ns, 1, Q_STRIP), F32), pltpu.VMEM((2, ns, 1, Q_STRIP), F32),
                        pltpu.VMEM((2, ns, V_DIM, Q_STRIP), F32), pltpu.VMEM((2, 2, ns, tk, Q_STRIP), F32),
                        pltpu.VMEM((2, 2, ns, tk, Q_STRIP), BF16), pltpu.VMEM((2, 2, ns, 1, Q_STRIP), F32)],
        compiler_params=_params(("parallel", "parallel", "arbitrary")),
        name="flash_prompt",
    )(lam_q, lam_k, subln_g.reshape(V_DIM, 1), q3, k3, vt4)
    return out.reshape(batch * seq, D_MODEL)


def _attn_sample_kernel(lq_ref, lk_ref, g_ref, q_ref, kn_ref, vn_ref, kc_ref, vc_ref, o_ref, *, lam_init):
    q = q_ref[...]
    kc = kc_ref[...].astype(BF16)
    vc = vc_ref[...].astype(BF16)
    kn = kn_ref[...]
    vn = vn_ref[...]
    accs, ls = [], []
    for mp in range(2):
        sl = slice(mp * HEAD_DIM, (mp + 1) * HEAD_DIM)
        s_past = _nt_dot(q[:, sl], kc[:, sl])
        s_new = _nt_dot(q[:, sl], kn[:, sl])
        mx = jnp.maximum(jnp.max(s_past, axis=-1, keepdims=True), jnp.max(s_new, axis=-1, keepdims=True))
        p_past = jnp.exp2(s_past - mx)
        p_new = jnp.exp2(s_new - mx)
        ls.append(jnp.sum(p_past, axis=-1, keepdims=True) + jnp.sum(p_new, axis=-1, keepdims=True))
        accs.append(jnp.dot(p_past.astype(BF16), vc, preferred_element_type=F32)
                    + jnp.dot(p_new.astype(BF16), vn, preferred_element_type=F32))
    lam = _lambda_value(lq_ref, lk_ref, lam_init)
    _subln_store(o_ref, accs[0], ls[0], accs[1], ls[1], lam, g_ref, lam_init)


def _attn_sample(q, k_new, v_new, cache_k, cache_v, lam_q, lam_k, subln_g, *, n_streams, dec_seq, past_len,
                 lam_init):
    kern = functools.partial(_attn_sample_kernel, lam_init=lam_init)
    small = lambda shape: pl.BlockSpec(shape, lambda b, h: (0, 0))
    new = pl.BlockSpec((dec_seq, V_DIM), lambda b, h: (b, h))
    past = pl.BlockSpec((None, past_len, V_DIM), lambda b, h: (b, 0, h))
    return pl.pallas_call(
        kern,
        grid=(n_streams, N_HEADS),
        in_specs=[small((2, HEAD_DIM)), small((2, HEAD_DIM)), small((1, V_DIM)), new, new, new, past, past],
        out_specs=new,
        out_shape=jax.ShapeDtypeStruct((n_streams * dec_seq, D_MODEL), BF16),
        compiler_params=_params(("parallel", "parallel")),
        name="attn_sample",
    )(lam_q, lam_k, subln_g.reshape(1, V_DIM), q, k_new, v_new, cache_k, cache_v)


def _attn_branch_kernel(o_ref, w_ref, g_ref, out_ref):
    o = o_ref[...]
    for c in range(D_MODEL // V_DIM):
        sl = slice(c * V_DIM, (c + 1) * V_DIM)
        acc = jnp.dot(o, w_ref[:, sl], preferred_element_type=F32)
        out_ref[:, sl] = (g_ref[:, sl].astype(F32) * acc).astype(out_ref.dtype)


def _conv_branch_kernel(dw_ref, lng_ref, lnb_ref, w_ref, g_ref, ga_ref, out_ref, act_ref):
    x = dw_ref[...].astype(F32)
    mu = jnp.mean(x, axis=-1, keepdims=True)
    xc = x - mu
    var = jnp.mean(xc * xc, axis=-1, keepdims=True)
    y = xc * lax.rsqrt(var + LN_EPS) * lng_ref[...] + lnb_ref[...]
    act_ref[...] = (y * _sigmoid(y)).astype(BF16)
    for c in range(D_MODEL // V_DIM):
        sl = slice(c * V_DIM, (c + 1) * V_DIM)
        acc = jnp.dot(act_ref[...], w_ref[:, sl], preferred_element_type=F32)
        merged = ga_ref[:, sl].astype(F32) + g_ref[:, sl].astype(F32) * acc
        out_ref[:, sl] = merged.astype(out_ref.dtype)


def _router_combine(logits):
    lane = lax.broadcasted_iota(jnp.int32, logits.shape, 1)
    big = jnp.int32(ROUTER_LANES)
    gmask = lane < N_GROUPS
    gl = jnp.where(gmask, logits, -jnp.inf)
    gmax = jnp.max(gl, axis=-1, keepdims=True)
    grp = jnp.min(jnp.where(gl == gmax, lane, big), axis=-1, keepdims=True)
    denom = jnp.sum(jnp.where(gmask, jnp.exp(gl - gmax), 0.0), axis=-1, keepdims=True)
    p_grp = 1.0 / denom
    eidx = lane - N_GROUPS
    emask = (eidx >= 0) & (eidx < N_EXPERTS) & ((eidx // EXPERTS_PER_GROUP) == grp)
    ev = jnp.where(emask, logits, -jnp.inf)
    l1 = jnp.max(ev, axis=-1, keepdims=True)
    i1 = jnp.min(jnp.where(ev == l1, lane, big), axis=-1, keepdims=True)
    ev2 = jnp.where(lane == i1, -jnp.inf, ev)
    l2 = jnp.max(ev2, axis=-1, keepdims=True)
    i2 = jnp.min(jnp.where(ev2 == l2, lane, big), axis=-1, keepdims=True)
    e2 = jnp.exp(l2 - l1)
    w1 = p_grp / (1.0 + e2)
    w2 = p_grp * e2 / (1.0 + e2)
    comb = jnp.where(lane == i1, w1, 0.0) + jnp.where(lane == i2, w2, 0.0)
    return jnp.where(lane == 0, grp.astype(F32), comb)


def _out_proj_kernel(mg_ref, w_ref, x_ref, gffn_ref, wr_ref, br_ref, h_ref, hn_ref, comb_ref):
    mg = mg_ref[...]
    for c in range(D_MODEL // V_DIM):
        sl = slice(c * V_DIM, (c + 1) * V_DIM)
        h_ref[:, sl] = x_ref[:, sl] + jnp.dot(mg, w_ref[:, sl], preferred_element_type=F32)
    h = h_ref[...]
    ms = jnp.mean(h * h, axis=-1, keepdims=True)
    hn = h * lax.rsqrt(ms + RMS_EPS) * gffn_ref[...]
    hn_hi = hn.astype(BF16)
    hn_ref[:, :D_MODEL] = hn.astype(hn_ref.dtype)
    hn_lo = (hn - hn_hi.astype(F32)).astype(BF16)
    hw = jnp.dot(hn_hi, wr_ref[...], preferred_element_type=F32)
    lw = jnp.dot(hn_lo, wr_ref[:, :ROUTER_LANES], preferred_element_type=F32)
    logits = hw[:, :ROUTER_LANES] + hw[:, ROUTER_LANES:] + lw + br_ref[...]
    comb = _router_combine(logits)
    comb_ref[...] = comb
    if hn_ref.shape[1] > D_MODEL:
        hn_ref[:, D_MODEL:] = comb


def _token_spec(tm, width=D_MODEL, col=0):
    return pl.BlockSpec((tm, width), lambda i: (i, col))


def _full_spec(shape):
    return pl.BlockSpec(shape, lambda i: (0,) * len(shape))


def _attn_branch(o, w_attn_bf, gates, *, tm):
    m = o.shape[0]
    return pl.pallas_call(
        _attn_branch_kernel,
        grid=(m // tm,),
        in_specs=[_token_spec(tm), _full_spec((ATTN_W, D_MODEL)), _token_spec(tm, D_MODEL, 0)],
        out_specs=_token_spec(tm),
        out_shape=jax.ShapeDtypeStruct((m, D_MODEL), BF16),
        compiler_params=_params(("parallel",)),
        name="attn_branch",
    )(o, w_attn_bf, gates)


def _conv_branch(dw, ln_g, ln_b, w_conv_bf, gates, ga, *, tm):
    m = dw.shape[0]
    return pl.pallas_call(
        _conv_branch_kernel,
        grid=(m // tm,),
        in_specs=[_token_spec(tm), _full_spec((1, C_CONV)), _full_spec((1, C_CONV)),
                  _full_spec((C_CONV, D_MODEL)), _token_spec(tm, D_MODEL, 1), _token_spec(tm)],
        out_specs=_token_spec(tm),
        out_shape=jax.ShapeDtypeStruct((m, D_MODEL), BF16),
        scratch_shapes=[pltpu.VMEM((tm, C_CONV), BF16)],
        compiler_params=_params(("parallel",)),
        name="conv_branch",
    )(dw, ln_g.reshape(1, C_CONV), ln_b.reshape(1, C_CONV), w_conv_bf, gates, ga)


def _out_proj(merged, w_out_bf, x, g_ffn, w_router, b_router, *, tm, with_router_lanes):
    m = merged.shape[0]
    hn_width = D_MODEL + ROUTER_LANES if with_router_lanes else D_MODEL
    hn_dtype = F32 if with_router_lanes else BF16
    return pl.pallas_call(
        _out_proj_kernel,
        grid=(m // tm,),
        in_specs=[_token_spec(tm), _full_spec((D_MODEL, D_MODEL)), _token_spec(tm), _full_spec((1, D_MODEL)),
                  _full_spec((D_MODEL, 2 * ROUTER_LANES)), _full_spec((1, ROUTER_LANES))],
        out_specs=[_token_spec(tm), _token_spec(tm, hn_width), _token_spec(tm, ROUTER_LANES)],
        out_shape=[jax.ShapeDtypeStruct((m, D_MODEL), F32), jax.ShapeDtypeStruct((m, hn_width), hn_dtype),
                   jax.ShapeDtypeStruct((m, ROUTER_LANES), F32)],
        compiler_params=_params(("parallel",)),
        name="out_proj_router",
    )(merged, w_out_bf, x, g_ffn.reshape(1, D_MODEL), w_router, b_router)


def _moe_dense_kernel(hn_ref, h_ref, comb_ref, wg_ref, wu_ref, wd_ref, gfin_ref, y_ref, acc_ref):
    e = pl.program_id(1)

    @pl.when(e == 0)
    def _():
        acc_ref[...] = h_ref[...]

    x = hn_ref[...]
    gate = jnp.dot(x, wg_ref[...], preferred_element_type=F32)
    up = jnp.dot(x, wu_ref[...], preferred_element_type=F32)
    comb = comb_ref[...]
    lane = lax.broadcasted_iota(jnp.int32, comb.shape, 1)
    c = jnp.sum(jnp.where(lane == e + N_GROUPS, comb, 0.0), axis=-1, keepdims=True)
    hid = (gate * _sigmoid(gate) * up * c).astype(BF16)
    acc_ref[...] += jnp.dot(hid, wd_ref[...], preferred_element_type=F32)

    @pl.when(e == N_EXPERTS - 1)
    def _():
        y = acc_ref[...]
        ms = jnp.mean(y * y, axis=-1, keepdims=True)
        y_ref[...] = y * lax.rsqrt(ms + RMS_EPS) * gfin_ref[...]


def _moe_dense(hn, h, comb, wg_bf, wu_bf, wd_bf, g_final, *, tm):
    m = hn.shape[0]
    return pl.pallas_call(
        _moe_dense_kernel,
        grid=(m // tm, N_EXPERTS),
        in_specs=[pl.BlockSpec((tm, D_MODEL), lambda i, e: (i, 0)),
                  pl.BlockSpec((tm, D_MODEL), lambda i, e: (i, 0)),
                  pl.BlockSpec((tm, ROUTER_LANES), lambda i, e: (i, 0)),
                  pl.BlockSpec((None, D_MODEL, D_EXPERT), lambda i, e: (e, 0, 0)),
                  pl.BlockSpec((None, D_MODEL, D_EXPERT), lambda i, e: (e, 0, 0)),
                  pl.BlockSpec((None, D_EXPERT, D_MODEL), lambda i, e: (e, 0, 0)),
                  pl.BlockSpec((1, D_MODEL), lambda i, e: (0, 0))],
        out_specs=pl.BlockSpec((tm, D_MODEL), lambda i, e: (i, 0)),
        out_shape=jax.ShapeDtypeStruct((m, D_MODEL), F32),
        scratch_shapes=[pltpu.VMEM((tm, D_MODEL), F32)],
        compiler_params=_params(("parallel", "arbitrary")),
        name="moe_dense",
    )(hn, h, comb, wg_bf, wu_bf, wd_bf, g_final.reshape(1, D_MODEL))


ATTN_W = N_HEADS * V_DIM


def _moe_group_kernel(tg_ref, src_ref, hx_hbm, wg_ref, wu_ref, wd_ref, y_ref, xbuf, xb_ref, comb_scr, sems, *, rows):
    i = pl.program_id(0)
    e = pl.program_id(1)
    n_tiles = pl.num_programs(0)
    slot = i % 2
    nslot = 1 - slot
    quarter = rows // EXPERTS_PER_GROUP

    def row_copy(tok, slot_, r):
        return pltpu.make_async_copy(hx_hbm.at[pl.ds(tok, 1)], xbuf.at[slot_, pl.ds(r, 1)], sems.at[slot_])

    def tile_wait(slot_):
        pltpu.make_async_copy(hx_hbm.at[pl.ds(0, rows)], xbuf.at[slot_], sems.at[slot_]).wait()

    @pl.when((i == 0) & (e == 0))
    def _():
        def body(r, carry):
            row_copy(src_ref[r], 0, r).start()
            return carry

        lax.fori_loop(0, rows, body, 0, unroll=8)

    nxt = jnp.where(i + 1 == n_tiles, 0, i + 1)
    base = nxt * rows + e * quarter
    for r in range(quarter):
        row_copy(src_ref[base + r], nslot, e * quarter + r).start()

    @pl.when(e == 0)
    def _():
        tile_wait(slot)
        xb_ref[...] = xbuf[slot, :, :D_MODEL].astype(BF16)
        comb_scr[...] = xbuf[slot, :, D_MODEL:]
        y_ref[...] = jnp.zeros(y_ref.shape, F32)

    x = xb_ref[...]
    gate = jnp.dot(x, wg_ref[...], preferred_element_type=F32)
    up = jnp.dot(x, wu_ref[...], preferred_element_type=F32)
    comb = comb_scr[...]
    lane = lax.broadcasted_iota(jnp.int32, comb.shape, 1)
    expert_lane = N_GROUPS + tg_ref[i] * EXPERTS_PER_GROUP + e
    c = jnp.sum(jnp.where(lane == expert_lane, comb, 0.0), axis=-1, keepdims=True)
    hid = (gate * _sigmoid(gate) * up * c).astype(BF16)
    y_ref[...] += jnp.dot(hid, wd_ref[...], preferred_element_type=F32)

    @pl.when((i == n_tiles - 1) & (e == EXPERTS_PER_GROUP - 1))
    def _():
        tile_wait(nslot)


def _moe_group(tile_group, src, hx, wg_bf, wu_bf, wd_bf, *, rows):
    p = src.shape[0]
    w_idx = lambda i, e, tg, sr: (tg[i] * EXPERTS_PER_GROUP + e, 0, 0)
    grid_spec = pltpu.PrefetchScalarGridSpec(
        num_scalar_prefetch=2,
        grid=(p // rows, EXPERTS_PER_GROUP),
        in_specs=[pl.BlockSpec(memory_space=pl.ANY),
                  pl.BlockSpec((None, D_MODEL, D_EXPERT), w_idx),
                  pl.BlockSpec((None, D_MODEL, D_EXPERT), w_idx),
                  pl.BlockSpec((None, D_EXPERT, D_MODEL), w_idx)],
        out_specs=pl.BlockSpec((rows, D_MODEL), lambda i, e, tg, sr: (i, 0)),
        scratch_shapes=[pltpu.VMEM((2, rows, D_MODEL + ROUTER_LANES), F32),
                        pltpu.VMEM((rows, D_MODEL), BF16),
                        pltpu.VMEM((rows, ROUTER_LANES), F32),
                        pltpu.SemaphoreType.DMA((2,))],
    )
    return pl.pallas_call(
        functools.partial(_moe_group_kernel, rows=rows),
        grid_spec=grid_spec,
        out_shape=jax.ShapeDtypeStruct((p, D_MODEL), F32),
        compiler_params=_params(("arbitrary", "arbitrary")),
        name="moe_group",
    )(tile_group, src, hx, wg_bf, wu_bf, wd_bf)


def _moe_finish_kernel(dest_ref, ys_hbm, h_ref, g_ref, y_ref, buf_ref, sems, *, rows):
    i = pl.program_id(0)
    n_tiles = pl.num_programs(0)
    slot = i % 2
    nslot = 1 - slot

    def row_copy(p, slot_, r):
        return pltpu.make_async_copy(ys_hbm.at[pl.ds(p, 1)], buf_ref.at[slot_, pl.ds(r, 1)], sems.at[slot_])

    def tile_wait(slot_):
        pltpu.make_async_copy(ys_hbm.at[pl.ds(0, rows)], buf_ref.at[slot_], sems.at[slot_]).wait()

    @pl.when(i == 0)
    def _():
        def body(r, carry):
            row_copy(dest_ref[r], 0, r).start()
            return carry

        lax.fori_loop(0, rows, body, 0, unroll=8)

    nxt = jnp.where(i + 1 == n_tiles, 0, i + 1)
    for r in range(rows):
        row_copy(dest_ref[nxt * rows + r], nslot, r).start()

    tile_wait(slot)
    y = h_ref[...] + buf_ref[slot]
    ms = jnp.mean(y * y, axis=-1, keepdims=True)
    y_ref[...] = y * lax.rsqrt(ms + RMS_EPS) * g_ref[...]

    @pl.when(i == n_tiles - 1)
    def _():
        tile_wait(nslot)


def _moe_finish(dest, y_sorted, h, g_final, *, rows):
    m = h.shape[0]
    grid_spec = pltpu.PrefetchScalarGridSpec(
        num_scalar_prefetch=1,
        grid=(m // rows,),
        in_specs=[pl.BlockSpec(memory_space=pl.ANY),
                  pl.BlockSpec((rows, D_MODEL), lambda i, d: (i, 0)),
                  pl.BlockSpec((1, D_MODEL), lambda i, d: (0, 0))],
        out_specs=pl.BlockSpec((rows, D_MODEL), lambda i, d: (i, 0)),
        scratch_shapes=[pltpu.VMEM((2, rows, D_MODEL), F32), pltpu.SemaphoreType.DMA((2,))],
    )
    return pl.pallas_call(
        functools.partial(_moe_finish_kernel, rows=rows),
        grid_spec=grid_spec,
        out_shape=jax.ShapeDtypeStruct((m, D_MODEL), F32),
        compiler_params=_params(("arbitrary",)),
        name="moe_finish",
    )(dest, y_sorted, h, g_final.reshape(1, D_MODEL))


def _moe_routed(hx, h, comb, wg_bf, wu_bf, wd_bf, g_final, *, rows):
    m = hx.shape[0]
    grp = comb[:, 0].astype(jnp.int32)
    onehot = (grp[:, None] == jnp.arange(N_GROUPS, dtype=jnp.int32)[None, :]).astype(jnp.int32)
    rank = jnp.cumsum(onehot, axis=0) - onehot
    counts = jnp.sum(onehot, axis=0)
    padded = ((counts + rows - 1) // rows) * rows
    seg_end = jnp.cumsum(padded)
    seg_start = seg_end - padded
    dest = jnp.sum(onehot * (seg_start[None, :] + rank), axis=1).astype(jnp.int32)
    p_rows = m + N_GROUPS * rows
    src = jnp.zeros((p_rows,), jnp.int32).at[dest].set(jnp.arange(m, dtype=jnp.int32))
    tile_start = jnp.arange(p_rows // rows, dtype=jnp.int32) * rows
    tile_group = jnp.minimum(jnp.sum((tile_start[:, None] >= seg_end[None, :]).astype(jnp.int32), axis=1),
                             N_GROUPS - 1).astype(jnp.int32)
    y_sorted = _moe_group(tile_group, src, hx, wg_bf, wu_bf, wd_bf, rows=rows)
    return _moe_finish(dest, y_sorted, h, g_final, rows=rows)


def _rope_tables(pos):
    half = HEAD_DIM // 2
    inv_freq = ROPE_THETA ** (-jnp.arange(half, dtype=F32) / half)
    ang = pos.astype(F32)[:, None] * inv_freq[None, :]
    cos, sin = jnp.cos(ang), jnp.sin(ang)
    return jnp.concatenate([cos, cos], axis=-1), jnp.concatenate([-sin, sin], axis=-1)


def _layer_tokens(x, pos_tables, init_state, attn_fn, wts, *, tm, n_seq, seq_len, tiles_per_seq, v_kind, routed):
    cos_t, sin_t = pos_tables
    xn = _rmsnorm(x, wts["norm_mix_g"], tm)
    (q,) = _proj_rope(xn, wts["w_in"], 0, cos_t, sin_t, tm=tm, rope=True, scale=SCALE * LOG2_E,
                      outs=(("tokens", BF16),), name="proj_q")
    k_f32, k_bf = _proj_rope(xn, wts["w_in"], 1, cos_t, sin_t, tm=tm, rope=True, scale=1.0,
                             outs=(("tokens", F32), ("tokens", BF16)), name="proj_k")
    v_f32, v_bf = _proj_rope(xn, wts["w_in"], 2, cos_t, sin_t, tm=tm, rope=False, scale=1.0,
                             outs=(("tokens", F32), (v_kind, BF16)), name="proj_v")
    dw, new_conv = _proj_glu_conv(xn, wts["w_glu"], init_state, wts["w_dw"], wts["b_dw"],
                                  n_seq=n_seq, seq_len=seq_len, tiles_per_seq=tiles_per_seq)
    gates = _proj_gate(xn, wts["w_in"], 5, wts["b_gate"], tm=tm)
    o = attn_fn(q, k_bf, v_bf)
    ga = _attn_branch(o, wts["w_attn"], gates, tm=tm)
    merged = _conv_branch(dw, wts["conv_ln_g"], wts["conv_ln_b"], wts["w_conv"], gates, ga, tm=tm)
    h, hn, comb = _out_proj(merged, wts["w_out"], x, wts["norm_ffn_g"], wts["w_router"], wts["b_router"], tm=tm,
                            with_router_lanes=routed)
    moe = functools.partial(_moe_routed, rows=MOE_ROWS) if routed else functools.partial(_moe_dense, tm=tm)
    y = moe(hn, h, comb, wts["w_eg"], wts["w_eu"], wts["w_ed"], wts["norm_final_g"])
    return y, k_f32, v_f32, new_conv


def kernel(x_prompt, x_sample, cache_k, cache_v, state_conv, norm_mix_g, w_in, b_gate, lambda_q, lambda_k, subln_g, w_attn_branch, w_dw, b_dw, conv_ln_g, conv_ln_b, w_conv_branch, w_out, norm_ffn_g, w_router_group, b_router_group, w_router_expert, b_router_expert, w_exp_gate, w_exp_up, w_exp_down, norm_final_g):
    batch, seq, _ = x_prompt.shape
    n_streams, dec_seq, _ = x_sample.shape
    depth, _, past_len = cache_k.shape[:3]
    assert depth == 1
    assert past_len % CHUNK == 0 and dec_seq <= CHUNK
    l = 0
    lam_init = 0.8 - 0.6 * math.exp(-0.3 * l)

    pad_r = ROUTER_LANES - N_GROUPS - N_EXPERTS
    w_router = jnp.concatenate([w_router_group[l], w_router_expert[l], jnp.zeros((D_MODEL, pad_r), F32)], axis=1)
    b_router = jnp.concatenate([b_router_group[l], b_router_expert[l], jnp.zeros((pad_r,), F32)]).reshape(1, -1)
    w_router_hi = w_router.astype(BF16)
    w_router_lo = (w_router - w_router_hi.astype(F32)).astype(BF16)
    w_router = jnp.concatenate([w_router_hi, w_router_lo], axis=1)
    w_in_bf = w_in[l].astype(BF16)
    n_chunks = C_CONV // LANES
    glu_cols = N_HEADS * (4 * HEAD_DIM + V_DIM)
    w_val = w_in_bf[:, glu_cols:glu_cols + C_CONV].reshape(D_MODEL, n_chunks, 1, LANES)
    w_gat = w_in_bf[:, glu_cols + C_CONV:glu_cols + 2 * C_CONV].reshape(D_MODEL, n_chunks, 1, LANES)
    w_glu = jnp.concatenate([w_val, w_gat], axis=2).reshape(D_MODEL, n_chunks // GLU_CHUNKS, GLU_CHUNKS * 2 * LANES)
    w_glu = w_glu.transpose(1, 0, 2)
    wts = dict(
        norm_mix_g=norm_mix_g[l], w_in=w_in_bf, w_glu=w_glu, b_gate=b_gate[l],
        w_attn=w_attn_branch[l].astype(BF16), w_dw=w_dw[l], b_dw=b_dw[l],
        conv_ln_g=conv_ln_g[l], conv_ln_b=conv_ln_b[l], w_conv=w_conv_branch[l].astype(BF16),
        w_out=w_out[l].astype(BF16), norm_ffn_g=norm_ffn_g[l], w_router=w_router, b_router=b_router,
        w_eg=w_exp_gate[l].astype(BF16), w_eu=w_exp_up[l].astype(BF16), w_ed=w_exp_down[l].astype(BF16),
        norm_final_g=norm_final_g)
    lam_q, lam_k, sub_g = lambda_q[l], lambda_k[l], subln_g[l]

    tm = 512
    tabs_p = _rope_tables(jnp.arange(seq, dtype=jnp.int32))
    init_p = jnp.zeros((batch, HIST_ROWS, C_CONV), F32)
    attn_p = functools.partial(_flash_prompt, lam_q=lam_q, lam_k=lam_k, subln_g=sub_g,
                               batch=batch, seq=seq, tq=2 * tm, lam_init=lam_init)
    y_p, k_p, v_p, conv_p = _layer_tokens(
        x_prompt.reshape(batch * seq, D_MODEL), tabs_p, init_p, attn_p, wts,
        tm=tm, n_seq=1, seq_len=tm, tiles_per_seq=seq // tm, v_kind="head_transposed", routed=True)

    cos_s, sin_s = _rope_tables(past_len + jnp.arange(dec_seq, dtype=jnp.int32))
    tabs_s = (jnp.tile(cos_s, (n_streams, 1)), jnp.tile(sin_s, (n_streams, 1)))
    init_s = jnp.pad(state_conv[l], ((0, 0), (HIST_OFF, 0), (0, 0)))
    ck = cache_k[l].reshape(n_streams, past_len, D_MODEL)
    cv = cache_v[l].reshape(n_streams, past_len, D_MODEL)
    attn_s = functools.partial(_attn_sample, cache_k=ck, cache_v=cv, lam_q=lam_q, lam_k=lam_k, subln_g=sub_g,
                               n_streams=n_streams, dec_seq=dec_seq, past_len=past_len, lam_init=lam_init)
    m_s = n_streams * dec_seq
    y_s, k_s, v_s, conv_s = _layer_tokens(
        x_sample.reshape(m_s, D_MODEL), tabs_s, init_s, attn_s, wts,
        tm=m_s, n_seq=n_streams, seq_len=dec_seq, tiles_per_seq=1, v_kind="tokens", routed=False)

    return (y_p.reshape(batch, seq, D_MODEL),
            y_s.reshape(n_streams, dec_seq, D_MODEL),
            k_p.reshape(1, batch, seq, N_HEADS, 2, HEAD_DIM),
            v_p.reshape(1, batch, seq, N_HEADS, V_DIM),
            conv_p.reshape(1, batch, CONV_K - 1, C_CONV),
            k_s.reshape(1, n_streams, dec_seq, N_HEADS, 2, HEAD_DIM),
            v_s.reshape(1, n_streams, dec_seq, N_HEADS, V_DIM),
            conv_s.reshape(1, n_streams, CONV_K - 1, C_CONV))
```

```python
import functools
import math

import jax
import jax.numpy as jnp
from jax import lax
from jax.experimental import pallas as pl
from jax.experimental.pallas import tpu as pltpu

F32 = jnp.float32
BF16 = jnp.bfloat16

D_MODEL = 2048
CHUNK = 64
HEAD_DIM = 128
N_HEADS = D_MODEL // (2 * HEAD_DIM)
V_DIM = 2 * HEAD_DIM
C_CONV = D_MODEL
CONV_K = 31
N_GROUPS = 4
EXPERTS_PER_GROUP = 4
N_EXPERTS = N_GROUPS * EXPERTS_PER_GROUP
D_EXPERT = D_MODEL // 4
ROPE_THETA = 10000.0
RMS_EPS = 1e-6
LN_EPS = 1e-5
NEG_INF = -1e30
SCALE = HEAD_DIM ** -0.5
LOG2_E = math.log2(math.e)
Q_STRIP = 256

LANES = 128
SUBLANES = 8
GLU_CHUNKS = 4
MOE_ROWS = 512
HIST_ROWS = 32
HIST_OFF = HIST_ROWS - (CONV_K - 1)
ROUTER_LANES = 128
VMEM_LIMIT = 56 * 1024 * 1024


def _params(sem):
    return pltpu.CompilerParams(dimension_semantics=sem, vmem_limit_bytes=VMEM_LIMIT)


def _sigmoid(x):
    return 1.0 / (1.0 + jnp.exp(-x))


def _rmsnorm_kernel(x_ref, g_ref, o_ref):
    x = x_ref[...]
    ms = jnp.mean(x * x, axis=-1, keepdims=True)
    o_ref[...] = (x * lax.rsqrt(ms + RMS_EPS) * g_ref[...]).astype(o_ref.dtype)


def _rmsnorm(x, g, tm):
    m = x.shape[0]
    return pl.pallas_call(
        _rmsnorm_kernel,
        grid=(m // tm,),
        in_specs=[pl.BlockSpec((tm, D_MODEL), lambda i: (i, 0)),
                  pl.BlockSpec((1, D_MODEL), lambda i: (0, 0))],
        out_specs=pl.BlockSpec((tm, D_MODEL), lambda i: (i, 0)),
        out_shape=jax.ShapeDtypeStruct((m, D_MODEL), BF16),
        compiler_params=_params(("parallel",)),
        name="rmsnorm_in",
    )(x, g.reshape(1, D_MODEL))


def _proj_rope_kernel(x_ref, w_ref, cos_ref, sin_ref, *out_refs, scale, rope, tn, out_kinds):
    x = x_ref[...]
    for c in range(tn // V_DIM):
        acc = jnp.dot(x, w_ref[:, c * V_DIM:(c + 1) * V_DIM], preferred_element_type=F32)
        if rope:
            cs = cos_ref[...]
            sn = sin_ref[...]
            halves = []
            for m in range(2):
                a = acc[:, m * HEAD_DIM:(m + 1) * HEAD_DIM]
                halves.append(a * cs + pltpu.roll(a, HEAD_DIM // 2, axis=1) * sn)
            acc = jnp.concatenate(halves, axis=1)
        if scale != 1.0:
            acc = acc * scale
        for o_ref, kind in zip(out_refs, out_kinds):
            if kind == "head_transposed":
                o_ref[c] = acc.T.astype(o_ref.dtype)
            elif kind == "map_rows":
                for m in range(2):
                    rows = pl.ds(2 * c + m, x.shape[0], stride=2 * N_HEADS)
                    o_ref[rows, :] = acc[:, m * HEAD_DIM:(m + 1) * HEAD_DIM].astype(o_ref.dtype)
            else:
                o_ref[:, c * V_DIM:(c + 1) * V_DIM] = acc.astype(o_ref.dtype)


def _proj_rope(xn, w_bf, col_block, cos_t, sin_t, *, tm, rope, scale, outs, name):
    m = xn.shape[0]
    tn = D_MODEL
    t_tiles = cos_t.shape[0] // tm
    kinds = tuple(k for k, _ in outs)
    kern = functools.partial(_proj_rope_kernel, scale=scale, rope=rope, tn=tn, out_kinds=kinds)
    out_specs, out_shape = [], []
    for kind, dt in outs:
        if kind == "head_transposed":
            out_specs.append(pl.BlockSpec((N_HEADS, None, V_DIM, tm), lambda i: (0, i, 0, 0)))
            out_shape.append(jax.ShapeDtypeStruct((N_HEADS, m // tm, V_DIM, tm), dt))
        elif kind == "map_rows":
            out_specs.append(pl.BlockSpec((tm * 2 * N_HEADS, HEAD_DIM), lambda i: (i, 0)))
            out_shape.append(jax.ShapeDtypeStruct((m * 2 * N_HEADS, HEAD_DIM), dt))
        else:
            out_specs.append(pl.BlockSpec((tm, tn), lambda i: (i, 0)))
            out_shape.append(jax.ShapeDtypeStruct((m, tn), dt))
    return pl.pallas_call(
        kern,
        grid=(m // tm,),
        in_specs=[pl.BlockSpec((tm, D_MODEL), lambda i: (i, 0)),
                  pl.BlockSpec((D_MODEL, tn), lambda i: (0, col_block)),
                  pl.BlockSpec((tm, HEAD_DIM), lambda i: (i % t_tiles, 0)),
                  pl.BlockSpec((tm, HEAD_DIM), lambda i: (i % t_tiles, 0))],
        out_specs=out_specs,
        out_shape=out_shape,
        compiler_params=_params(("parallel",)),
        name=name,
    )(xn, w_bf, cos_t, sin_t)


def _proj_gate_kernel(x_ref, w_ref, b_ref, o_ref, *, tn):
    x = x_ref[...]
    for c in range(tn // V_DIM):
        sl = slice(c * V_DIM, (c + 1) * V_DIM)
        acc = jnp.dot(x, w_ref[:, sl], preferred_element_type=F32) + b_ref[:, sl]
        o_ref[:, sl] = _sigmoid(acc).astype(o_ref.dtype)


def _proj_gate(xn, w_bf, col_block, b_gate, *, tm):
    m = xn.shape[0]
    tn = D_MODEL
    kern = functools.partial(_proj_gate_kernel, tn=tn)
    return pl.pallas_call(
        kern,
        grid=(2, m // tm),
        in_specs=[pl.BlockSpec((tm, D_MODEL), lambda j, i: (i, 0)),
                  pl.BlockSpec((D_MODEL, tn), lambda j, i: (0, col_block + j)),
                  pl.BlockSpec((1, tn), lambda j, i: (0, j))],
        out_specs=pl.BlockSpec((tm, tn), lambda j, i: (i, j)),
        out_shape=jax.ShapeDtypeStruct((m, 2 * D_MODEL), BF16),
        compiler_params=_params(("parallel", "parallel")),
        name="proj_gate",
    )(xn, w_bf, b_gate.reshape(1, 2 * D_MODEL))


def _odd_stride(seq_len):
    s = -(-seq_len // SUBLANES)
    return s if s % 2 == 1 else s + 1


def _proj_glu_conv_kernel(x_ref, w_ref, init_ref, wdw_ref, bdw_ref, dw_ref, state_ref,
                          pad_ref, out_scr, carry_ref, *, n_seq, seq_len, tiles_per_seq):
    tile = pl.program_id(1)
    stride = _odd_stride(seq_len)
    pad_rows = pad_ref.shape[2]
    group = next(g for g in (13, 9, 5, 3, 1) if stride % g == 0)
    x = x_ref[...]
    for q in range(GLU_CHUNKS):
        ls = slice(q * LANES, (q + 1) * LANES)
        ab = jnp.dot(x, w_ref[:, q * 2 * LANES:(q + 1) * 2 * LANES], preferred_element_type=F32)
        glu = ab[:, :LANES] * _sigmoid(ab[:, LANES:])
        for s in range(n_seq):
            if tiles_per_seq == 1:
                pad_ref[q, s, 0:HIST_ROWS, :] = init_ref[s, :, ls]
            else:
                first = (tile % tiles_per_seq) == 0
                pad_ref[q, s, 0:HIST_ROWS, :] = jnp.where(first, init_ref[s, :, ls], carry_ref[q])
            pad_ref[q, s, HIST_ROWS:HIST_ROWS + seq_len, :] = glu[s * seq_len:(s + 1) * seq_len, :]
            pad_ref[q, s, HIST_ROWS + seq_len:pad_rows, :] = jnp.zeros((pad_rows - HIST_ROWS - seq_len, LANES), F32)
            if tiles_per_seq != 1:
                carry_ref[q] = pad_ref[q, s, seq_len:seq_len + HIST_ROWS, :]
            state_ref[s, :, ls] = pad_ref[q, s, seq_len + HIST_OFF:seq_len + HIST_ROWS, :]

        bias = jnp.broadcast_to(bdw_ref[:, ls], (SUBLANES, LANES))
        for s in range(n_seq):
            for g0 in range(0, stride, group):
                accs = [bias] * group
                for j in range(CONV_K):
                    w = jnp.broadcast_to(wdw_ref[j:j + 1, ls], (SUBLANES, LANES))
                    for k in range(group):
                        rows = pl.ds(g0 + k + HIST_OFF + j, SUBLANES, stride=stride)
                        accs[k] = accs[k] + w * pad_ref[q, s, rows, :]
                for k in range(group):
                    out_scr[q, s, pl.ds(g0 + k, SUBLANES, stride=stride), :] = accs[k]
            dw_ref[s * seq_len:(s + 1) * seq_len, ls] = out_scr[q, s, 0:seq_len, :].astype(dw_ref.dtype)


def _proj_glu_conv(xn, w_glu, init_state, w_dw, b_dw, *, n_seq, seq_len, tiles_per_seq):
    m = xn.shape[0]
    tm = n_seq * seq_len
    uw = GLU_CHUNKS * LANES
    n_sequences = init_state.shape[0]
    stride = _odd_stride(seq_len)
    pad_rows = -(-(HIST_ROWS + SUBLANES * stride + SUBLANES) // SUBLANES) * SUBLANES
    kern = functools.partial(_proj_glu_conv_kernel, n_seq=n_seq, seq_len=seq_len, tiles_per_seq=tiles_per_seq)
    return pl.pallas_call(
        kern,
        grid=(C_CONV // uw, m // tm),
        in_specs=[pl.BlockSpec((tm, D_MODEL), lambda c, i: (i, 0)),
                  pl.BlockSpec((None, D_MODEL, 2 * uw), lambda c, i: (c, 0, 0)),
                  pl.BlockSpec((n_seq, HIST_ROWS, uw), lambda c, i: (i // tiles_per_seq, 0, c)),
                  pl.BlockSpec((CONV_K, uw), lambda c, i: (0, c)),
                  pl.BlockSpec((1, uw), lambda c, i: (0, c))],
        out_specs=[pl.BlockSpec((tm, uw), lambda c, i: (i, c)),
                   pl.BlockSpec((n_seq, CONV_K - 1, uw), lambda c, i: (i // tiles_per_seq, 0, c))],
        out_shape=[jax.ShapeDtypeStruct((m, C_CONV), BF16),
                   jax.ShapeDtypeStruct((n_sequences, CONV_K - 1, C_CONV), F32)],
        scratch_shapes=[pltpu.VMEM((GLU_CHUNKS, n_seq, pad_rows, LANES), F32),
                        pltpu.VMEM((GLU_CHUNKS, n_seq, SUBLANES * stride, LANES), F32),
                        pltpu.VMEM((GLU_CHUNKS, HIST_ROWS, LANES), F32)],
        compiler_params=_params(("parallel", "arbitrary")),
        name="proj_glu_conv",
    )(xn, w_glu, init_state, w_dw, b_dw.reshape(1, C_CONV))


def _lambda_value(lq_ref, lk_ref, lam_init):
    prod = lq_ref[...] * lk_ref[...]
    d = jnp.sum(prod, axis=1, keepdims=True)
    e = jnp.exp(d)
    return e[0:1, :] - e[1:2, :] + lam_init


def _subln_store(o_ref, acc0, l0, acc1, l1, lam, g_ref, lam_init):
    o = acc0 / l0 - lam * (acc1 / l1)
    ms = jnp.mean(o * o, axis=-1, keepdims=True)
    o = o * lax.rsqrt(ms + RMS_EPS) * g_ref[...] * (1.0 - lam_init)
    o_ref[...] = o.astype(o_ref.dtype)


def _nt_dot(a, b):
    return lax.dot_general(a, b, (((1,), (1,)), ((), ())), preferred_element_type=F32)


def _flash_prompt_kernel(lq_ref, lk_ref, g_ref, q_ref, k_ref, vt_ref, o_ref,
                         m_ref, l_ref, acc_ref, st_ref, p_ref, alpha_ref, *, tq, tk, lam_init):
    nd = tq // tk
    assert nd == 2
    qi = pl.program_id(2)
    q = q_ref[...]
    m_ref[...] = jnp.full(m_ref.shape, NEG_INF, F32)
    l_ref[...] = jnp.zeros(l_ref.shape, F32)
    acc_ref[...] = jnp.zeros(acc_ref.shape, F32)
    units = [(mp, c) for c in range(tq // Q_STRIP) for mp in range(2)]

    def scores(k, slot, diag, mp, c):
        sl = slice(mp * HEAD_DIM, (mp + 1) * HEAD_DIM)
        cs = slice(c * Q_STRIP, (c + 1) * Q_STRIP)
        st = _nt_dot(k[:, sl], q[cs, sl])
        if diag is not None:
            key_chunk = (lax.broadcasted_iota(jnp.int32, (tk, Q_STRIP), 0) + diag * tk) // CHUNK
            qry_chunk = (lax.broadcasted_iota(jnp.int32, (tk, Q_STRIP), 1) + c * Q_STRIP) // CHUNK
            st = jnp.where(key_chunk <= qry_chunk, st, NEG_INF)
        st_ref[slot, mp, c] = st

    def softmax(slot, mp, c):
        st = st_ref[slot, mp, c]
        m_old = m_ref[mp, c]
        m_new = jnp.maximum(m_old, jnp.max(st, axis=0, keepdims=True))
        alpha = jnp.exp2(m_old - m_new)
        p = jnp.exp2(st - m_new)
        l_ref[mp, c] = l_ref[mp, c] * alpha + jnp.sum(p, axis=0, keepdims=True)
        m_ref[mp, c] = m_new
        p_ref[slot, mp, c] = p.astype(BF16)
        alpha_ref[slot, mp, c] = alpha

    def pv(vt, slot, mp, c):
        acc_ref[mp, c] = (acc_ref[mp, c] * alpha_ref[slot, mp, c]
                          + jnp.dot(vt, p_ref[slot, mp, c], preferred_element_type=F32))

    def load_k(kidx):
        return k_ref[pl.ds(pl.multiple_of(kidx * tk, tk), tk), :]

    def key_of(i):
        return jnp.where(i < nd, nd * qi + i, i - nd)

    def iteration(s_slot, k_new, q_slot, vt, p_slot):
        for (mp, c) in units:
            softmax(s_slot, mp, c)
            scores(k_new, q_slot, None, mp, c)
            pv(vt, p_slot, mp, c)

    def softmax_all(slot):
        for (mp, c) in units:
            softmax(slot, mp, c)

    def pv_all(kidx, slot):
        vt = vt_ref[kidx]
        for (mp, c) in units:
            pv(vt, slot, mp, c)

    def masked_scores(slot, d):
        kd = load_k(nd * qi + d)
        for (mp, c) in units:
            scores(kd, slot, d, mp, c)

    masked_scores(0, 0)
    softmax_all(0)
    masked_scores(1, 1)

    def pair(jj, carry):
        j = 2 * jj + 2
        iteration(1, load_k(j - nd), 0, vt_ref[key_of(j - 2)], 0)
        iteration(0, load_k(j + 1 - nd), 1, vt_ref[key_of(j - 1)], 1)
        return carry

    lax.fori_loop(0, qi, pair, 0)
    n = nd * qi + nd
    softmax_all(1)
    pv_all(key_of(n - 2), 0)
    pv_all(key_of(n - 1), 1)

    lam = _lambda_value(lq_ref, lk_ref, lam_init)
    gain = g_ref[...] * (1.0 - lam_init)
    for c in range(tq // Q_STRIP):
        ot = acc_ref[0, c] * (1.0 / l_ref[0, c]) - lam * (acc_ref[1, c] * (1.0 / l_ref[1, c]))
        ms = jnp.mean(ot * ot, axis=0, keepdims=True)
        ot = ot * lax.rsqrt(ms + RMS_EPS) * gain
        o_ref[c * Q_STRIP:(c + 1) * Q_STRIP, :] = ot.T.astype(o_ref.dtype)


def _flash_prompt(q, k, vt4, lam_q, lam_k, subln_g, *, batch, seq, tq, lam_init):
    tk = vt4.shape[-1]
    nkb = seq // tk
    ns = tq // Q_STRIP
    q3 = q.reshape(batch, seq, D_MODEL)
    k3 = k.reshape(batch, seq, D_MODEL)
    kern = functools.partial(_flash_prompt_kernel, tq=tq, tk=tk, lam_init=lam_init)
    small = lambda shape: pl.BlockSpec(shape, lambda b, h, i: (0, 0))
    out = pl.pallas_call(
        kern,
        grid=(batch, N_HEADS, seq // tq),
        in_specs=[small((2, HEAD_DIM)), small((2, HEAD_DIM)), small((V_DIM, 1)),
                  pl.BlockSpec((None, tq, V_DIM), lambda b, h, i: (b, i, h)),
                  pl.BlockSpec((None, seq, V_DIM), lambda b, h, i: (b, 0, h)),
                  pl.BlockSpec((None, nkb, V_DIM, tk), lambda b, h, i: (h, b, 0, 0))],
        out_specs=pl.BlockSpec((None, tq, V_DIM), lambda b, h, i: (b, i, h)),
        out_shape=jax.ShapeDtypeStruct((batch, seq, D_MODEL), BF16),
        scratch_shapes=[pltpu.VMEM((2, ns, 1, Q_STRIP), F32), pltpu.VMEM((2, ns, 1, Q_STRIP), F32),
                        pltpu.VMEM((2, ns, V_DIM, Q_STRIP), F32), pltpu.VMEM((2, 2, ns, tk, Q_STRIP), F32),
                        pltpu.VMEM((2, 2, ns, tk, Q_STRIP), BF16), pltpu.VMEM((2, 2, ns, 1, Q_STRIP), F32)],
        compiler_params=_params(("parallel", "parallel", "arbitrary")),
        name="flash_prompt",
    )(lam_q, lam_k, subln_g.reshape(V_DIM, 1), q3, k3, vt4)
    return out.reshape(batch * seq, D_MODEL)


def _attn_sample_kernel(lq_ref, lk_ref, g_ref, q_ref, kn_ref, vn_ref, kc_ref, vc_ref, o_ref, *, lam_init):
    q = q_ref[...]
    kc = kc_ref[...].astype(BF16)
    vc = vc_ref[...].astype(BF16)
    kn = kn_ref[...]
    vn = vn_ref[...]
    accs, ls = [], []
    for mp in range(2):
        sl = slice(mp * HEAD_DIM, (mp + 1) * HEAD_DIM)
        s_past = _nt_dot(q[:, sl], kc[:, sl])
        s_new = _nt_dot(q[:, sl], kn[:, sl])
        mx = jnp.maximum(jnp.max(s_past, axis=-1, keepdims=True), jnp.max(s_new, axis=-1, keepdims=True))
        p_past = jnp.exp2(s_past - mx)
        p_new = jnp.exp2(s_new - mx)
        ls.append(jnp.sum(p_past, axis=-1, keepdims=True) + jnp.sum(p_new, axis=-1, keepdims=True))
        accs.append(jnp.dot(p_past.astype(BF16), vc, preferred_element_type=F32)
                    + jnp.dot(p_new.astype(BF16), vn, preferred_element_type=F32))
    lam = _lambda_value(lq_ref, lk_ref, lam_init)
    _subln_store(o_ref, accs[0], ls[0], accs[1], ls[1], lam, g_ref, lam_init)


def _attn_sample(q, k_new, v_new, cache_k, cache_v, lam_q, lam_k, subln_g, *, n_streams, dec_seq, past_len,
                 lam_init):
    kern = functools.partial(_attn_sample_kernel, lam_init=lam_init)
    small = lambda shape: pl.BlockSpec(shape, lambda b, h: (0, 0))
    new = pl.BlockSpec((dec_seq, V_DIM), lambda b, h: (b, h))
    past = pl.BlockSpec((None, past_len, V_DIM), lambda b, h: (b, 0, h))
    return pl.pallas_call(
        kern,
        grid=(n_streams, N_HEADS),
        in_specs=[small((2, HEAD_DIM)), small((2, HEAD_DIM)), small((1, V_DIM)), new, new, new, past, past],
        out_specs=new,
        out_shape=jax.ShapeDtypeStruct((n_streams * dec_seq, D_MODEL), BF16),
        compiler_params=_params(("parallel", "parallel")),
        name="attn_sample",
    )(lam_q, lam_k, subln_g.reshape(1, V_DIM), q, k_new, v_new, cache_k, cache_v)


def _attn_branch_kernel(o_ref, w_ref, g_ref, out_ref):
    o = o_ref[...]
    for c in range(D_MODEL // V_DIM):
        sl = slice(c * V_DIM, (c + 1) * V_DIM)
        acc = jnp.dot(o, w_ref[:, sl], preferred_element_type=F32)
        out_ref[:, sl] = (g_ref[:, sl].astype(F32) * acc).astype(out_ref.dtype)


def _conv_branch_kernel(dw_ref, lng_ref, lnb_ref, w_ref, g_ref, ga_ref, out_ref, act_ref):
    x = dw_ref[...].astype(F32)
    mu = jnp.mean(x, axis=-1, keepdims=True)
    xc = x - mu
    var = jnp.mean(xc * xc, axis=-1, keepdims=True)
    y = xc * lax.rsqrt(var + LN_EPS) * lng_ref[...] + lnb_ref[...]
    act_ref[...] = (y * _sigmoid(y)).astype(BF16)
    for c in range(D_MODEL // V_DIM):
        sl = slice(c * V_DIM, (c + 1) * V_DIM)
        acc = jnp.dot(act_ref[...], w_ref[:, sl], preferred_element_type=F32)
        merged = ga_ref[:, sl].astype(F32) + g_ref[:, sl].astype(F32) * acc
        out_ref[:, sl] = merged.astype(out_ref.dtype)


def _router_combine(logits):
    lane = lax.broadcasted_iota(jnp.int32, logits.shape, 1)
    big = jnp.int32(ROUTER_LANES)
    gmask = lane < N_GROUPS
    gl = jnp.where(gmask, logits, -jnp.inf)
    gmax = jnp.max(gl, axis=-1, keepdims=True)
    grp = jnp.min(jnp.where(gl == gmax, lane, big), axis=-1, keepdims=True)
    denom = jnp.sum(jnp.where(gmask, jnp.exp(gl - gmax), 0.0), axis=-1, keepdims=True)
    p_grp = 1.0 / denom
    eidx = lane - N_GROUPS
    emask = (eidx >= 0) & (eidx < N_EXPERTS) & ((eidx // EXPERTS_PER_GROUP) == grp)
    ev = jnp.where(emask, logits, -jnp.inf)
    l1 = jnp.max(ev, axis=-1, keepdims=True)
    i1 = jnp.min(jnp.where(ev == l1, lane, big), axis=-1, keepdims=True)
    ev2 = jnp.where(lane == i1, -jnp.inf, ev)
    l2 = jnp.max(ev2, axis=-1, keepdims=True)
    i2 = jnp.min(jnp.where(ev2 == l2, lane, big), axis=-1, keepdims=True)
    e2 = jnp.exp(l2 - l1)
    w1 = p_grp / (1.0 + e2)
    w2 = p_grp * e2 / (1.0 + e2)
    comb = jnp.where(lane == i1, w1, 0.0) + jnp.where(lane == i2, w2, 0.0)
    return jnp.where(lane == 0, grp.astype(F32), comb)


def _out_proj_kernel(mg_ref, w_ref, x_ref, gffn_ref, wr_ref, br_ref, h_ref, hn_ref, comb_ref):
    mg = mg_ref[...]
    for c in range(D_MODEL // V_DIM):
        sl = slice(c * V_DIM, (c + 1) * V_DIM)
        h_ref[:, sl] = x_ref[:, sl] + jnp.dot(mg, w_ref[:, sl], preferred_element_type=F32)
    h = h_ref[...]
    ms = jnp.mean(h * h, axis=-1, keepdims=True)
    hn = h * lax.rsqrt(ms + RMS_EPS) * gffn_ref[...]
    hn_hi = hn.astype(BF16)
    hn_ref[:, :D_MODEL] = hn.astype(hn_ref.dtype)
    hn_lo = (hn - hn_hi.astype(F32)).astype(BF16)
    hw = jnp.dot(hn_hi, wr_ref[...], preferred_element_type=F32)
    lw = jnp.dot(hn_lo, wr_ref[:, :ROUTER_LANES], preferred_element_type=F32)
    logits = hw[:, :ROUTER_LANES] + hw[:, ROUTER_LANES:] + lw + br_ref[...]
    comb = _router_combine(logits)
    comb_ref[...] = comb
    if hn_ref.shape[1] > D_MODEL:
        hn_ref[:, D_MODEL:] = comb


def _token_spec(tm, width=D_MODEL, col=0):
    return pl.BlockSpec((tm, width), lambda i: (i, col))


def _full_spec(shape):
    return pl.BlockSpec(shape, lambda i: (0,) * len(shape))


def _attn_branch(o, w_attn_bf, gates, *, tm):
    m = o.shape[0]
    return pl.pallas_call(
        _attn_branch_kernel,
        grid=(m // tm,),
        in_specs=[_token_spec(tm), _full_spec((ATTN_W, D_MODEL)), _token_spec(tm, D_MODEL, 0)],
        out_specs=_token_spec(tm),
        out_shape=jax.ShapeDtypeStruct((m, D_MODEL), BF16),
        compiler_params=_params(("parallel",)),
        name="attn_branch",
    )(o, w_attn_bf, gates)


def _conv_branch(dw, ln_g, ln_b, w_conv_bf, gates, ga, *, tm):
    m = dw.shape[0]
    return pl.pallas_call(
        _conv_branch_kernel,
        grid=(m // tm,),
        in_specs=[_token_spec(tm), _full_spec((1, C_CONV)), _full_spec((1, C_CONV)),
                  _full_spec((C_CONV, D_MODEL)), _token_spec(tm, D_MODEL, 1), _token_spec(tm)],
        out_specs=_token_spec(tm),
        out_shape=jax.ShapeDtypeStruct((m, D_MODEL), BF16),
        scratch_shapes=[pltpu.VMEM((tm, C_CONV), BF16)],
        compiler_params=_params(("parallel",)),
        name="conv_branch",
    )(dw, ln_g.reshape(1, C_CONV), ln_b.reshape(1, C_CONV), w_conv_bf, gates, ga)


def _out_proj(merged, w_out_bf, x, g_ffn, w_router, b_router, *, tm, with_router_lanes):
    m = merged.shape[0]
    hn_width = D_MODEL + ROUTER_LANES if with_router_lanes else D_MODEL
    hn_dtype = F32 if with_router_lanes else BF16
    return pl.pallas_call(
        _out_proj_kernel,
        grid=(m // tm,),
        in_specs=[_token_spec(tm), _full_spec((D_MODEL, D_MODEL)), _token_spec(tm), _full_spec((1, D_MODEL)),
                  _full_spec((D_MODEL, 2 * ROUTER_LANES)), _full_spec((1, ROUTER_LANES))],
        out_specs=[_token_spec(tm), _token_spec(tm, hn_width), _token_spec(tm, ROUTER_LANES)],
        out_shape=[jax.ShapeDtypeStruct((m, D_MODEL), F32), jax.ShapeDtypeStruct((m, hn_width), hn_dtype),
                   jax.ShapeDtypeStruct((m, ROUTER_LANES), F32)],
        compiler_params=_params(("parallel",)),
        name="out_proj_router",
    )(merged, w_out_bf, x, g_ffn.reshape(1, D_MODEL), w_router, b_router)


def _moe_dense_kernel(hn_ref, h_ref, comb_ref, wg_ref, wu_ref, wd_ref, gfin_ref, y_ref, acc_ref):
    e = pl.program_id(1)

    @pl.when(e == 0)
    def _():
        acc_ref[...] = h_ref[...]

    x = hn_ref[...]
    gate = jnp.dot(x, wg_ref[...], preferred_element_type=F32)
    up = jnp.dot(x, wu_ref[...], preferred_element_type=F32)
    comb = comb_ref[...]
    lane = lax.broadcasted_iota(jnp.int32, comb.shape, 1)
    c = jnp.sum(jnp.where(lane == e + N_GROUPS, comb, 0.0), axis=-1, keepdims=True)
    hid = (gate * _sigmoid(gate) * up * c).astype(BF16)
    acc_ref[...] += jnp.dot(hid, wd_ref[...], preferred_element_type=F32)

    @pl.when(e == N_EXPERTS - 1)
    def _():
        y = acc_ref[...]
        ms = jnp.mean(y * y, axis=-1, keepdims=True)
        y_ref[...] = y * lax.rsqrt(ms + RMS_EPS) * gfin_ref[...]


def _moe_dense(hn, h, comb, wg_bf, wu_bf, wd_bf, g_final, *, tm):
    m = hn.shape[0]
    return pl.pallas_call(
        _moe_dense_kernel,
        grid=(m // tm, N_EXPERTS),
        in_specs=[pl.BlockSpec((tm, D_MODEL), lambda i, e: (i, 0)),
                  pl.BlockSpec((tm, D_MODEL), lambda i, e: (i, 0)),
                  pl.BlockSpec((tm, ROUTER_LANES), lambda i, e: (i, 0)),
                  pl.BlockSpec((None, D_MODEL, D_EXPERT), lambda i, e: (e, 0, 0)),
                  pl.BlockSpec((None, D_MODEL, D_EXPERT), lambda i, e: (e, 0, 0)),
                  pl.BlockSpec((None, D_EXPERT, D_MODEL), lambda i, e: (e, 0, 0)),
                  pl.BlockSpec((1, D_MODEL), lambda i, e: (0, 0))],
        out_specs=pl.BlockSpec((tm, D_MODEL), lambda i, e: (i, 0)),
        out_shape=jax.ShapeDtypeStruct((m, D_MODEL), F32),
        scratch_shapes=[pltpu.VMEM((tm, D_MODEL), F32)],
        compiler_params=_params(("parallel", "arbitrary")),
        name="moe_dense",
    )(hn, h, comb, wg_bf, wu_bf, wd_bf, g_final.reshape(1, D_MODEL))


ATTN_W = N_HEADS * V_DIM


def _moe_group_kernel(tg_ref, src_ref, hx_hbm, wg_ref, wu_ref, wd_ref, y_ref, xbuf, xb_ref, comb_scr, sems, *, rows):
    i = pl.program_id(0)
    e = pl.program_id(1)
    n_tiles = pl.num_programs(0)
    slot = i % 2
    nslot = 1 - slot
    quarter = rows // EXPERTS_PER_GROUP

    def row_copy(tok, slot_, r):
        return pltpu.make_async_copy(hx_hbm.at[pl.ds(tok, 1)], xbuf.at[slot_, pl.ds(r, 1)], sems.at[slot_])

    def tile_wait(slot_):
        pltpu.make_async_copy(hx_hbm.at[pl.ds(0, rows)], xbuf.at[slot_], sems.at[slot_]).wait()

    @pl.when((i == 0) & (e == 0))
    def _():
        def body(r, carry):
            row_copy(src_ref[r], 0, r).start()
            return carry

        lax.fori_loop(0, rows, body, 0, unroll=8)

    nxt = jnp.where(i + 1 == n_tiles, 0, i + 1)
    base = nxt * rows + e * quarter
    for r in range(quarter):
        row_copy(src_ref[base + r], nslot, e * quarter + r).start()

    @pl.when(e == 0)
    def _():
        tile_wait(slot)
        xb_ref[...] = xbuf[slot, :, :D_MODEL].astype(BF16)
        comb_scr[...] = xbuf[slot, :, D_MODEL:]
        y_ref[...] = jnp.zeros(y_ref.shape, F32)

    x = xb_ref[...]
    gate = jnp.dot(x, wg_ref[...], preferred_element_type=F32)
    up = jnp.dot(x, wu_ref[...], preferred_element_type=F32)
    comb = comb_scr[...]
    lane = lax.broadcasted_iota(jnp.int32, comb.shape, 1)
    expert_lane = N_GROUPS + tg_ref[i] * EXPERTS_PER_GROUP + e
    c = jnp.sum(jnp.where(lane == expert_lane, comb, 0.0), axis=-1, keepdims=True)
    hid = (gate * _sigmoid(gate) * up * c).astype(BF16)
    y_ref[...] += jnp.dot(hid, wd_ref[...], preferred_element_type=F32)

    @pl.when((i == n_tiles - 1) & (e == EXPERTS_PER_GROUP - 1))
    def _():
        tile_wait(nslot)


def _moe_group(tile_group, src, hx, wg_bf, wu_bf, wd_bf, *, rows):
    p = src.shape[0]
    w_idx = lambda i, e, tg, sr: (tg[i] * EXPERTS_PER_GROUP + e, 0, 0)
    grid_spec = pltpu.PrefetchScalarGridSpec(
        num_scalar_prefetch=2,
        grid=(p // rows, EXPERTS_PER_GROUP),
        in_specs=[pl.BlockSpec(memory_space=pl.ANY),
                  pl.BlockSpec((None, D_MODEL, D_EXPERT), w_idx),
                  pl.BlockSpec((None, D_MODEL, D_EXPERT), w_idx),
                  pl.BlockSpec((None, D_EXPERT, D_MODEL), w_idx)],
        out_specs=pl.BlockSpec((rows, D_MODEL), lambda i, e, tg, sr: (i, 0)),
        scratch_shapes=[pltpu.VMEM((2, rows, D_MODEL + ROUTER_LANES), F32),
                        pltpu.VMEM((rows, D_MODEL), BF16),
                        pltpu.VMEM((rows, ROUTER_LANES), F32),
                        pltpu.SemaphoreType.DMA((2,))],
    )
    return pl.pallas_call(
        functools.partial(_moe_group_kernel, rows=rows),
        grid_spec=grid_spec,
        out_shape=jax.ShapeDtypeStruct((p, D_MODEL), F32),
        compiler_params=_params(("arbitrary", "arbitrary")),
        name="moe_group",
    )(tile_group, src, hx, wg_bf, wu_bf, wd_bf)


def _moe_finish_kernel(dest_ref, ys_hbm, h_ref, g_ref, y_ref, buf_ref, sems, *, rows):
    i = pl.program_id(0)
    n_tiles = pl.num_programs(0)
    slot = i % 2
    nslot = 1 - slot

    def row_copy(p, slot_, r):
        return pltpu.make_async_copy(ys_hbm.at[pl.ds(p, 1)], buf_ref.at[slot_, pl.ds(r, 1)], sems.at[slot_])

    def tile_wait(slot_):
        pltpu.make_async_copy(ys_hbm.at[pl.ds(0, rows)], buf_ref.at[slot_], sems.at[slot_]).wait()

    @pl.when(i == 0)
    def _():
        def body(r, carry):
            row_copy(dest_ref[r], 0, r).start()
            return carry

        lax.fori_loop(0, rows, body, 0, unroll=8)

    nxt = jnp.where(i + 1 == n_tiles, 0, i + 1)
    for r in range(rows):
        row_copy(dest_ref[nxt * rows + r], nslot, r).start()

    tile_wait(slot)
    y = h_ref[...] + buf_ref[slot]
    ms = jnp.mean(y * y, axis=-1, keepdims=True)
    y_ref[...] = y * lax.rsqrt(ms + RMS_EPS) * g_ref[...]

    @pl.when(i == n_tiles - 1)
    def _():
        tile_wait(nslot)


def _moe_finish(dest, y_sorted, h, g_final, *, rows):
    m = h.shape[0]
    grid_spec = pltpu.PrefetchScalarGridSpec(
        num_scalar_prefetch=1,
        grid=(m // rows,),
        in_specs=[pl.BlockSpec(memory_space=pl.ANY),
                  pl.BlockSpec((rows, D_MODEL), lambda i, d: (i, 0)),
                  pl.BlockSpec((1, D_MODEL), lambda i, d: (0, 0))],
        out_specs=pl.BlockSpec((rows, D_MODEL), lambda i, d: (i, 0)),
        scratch_shapes=[pltpu.VMEM((2, rows, D_MODEL), F32), pltpu.SemaphoreType.DMA((2,))],
    )
    return pl.pallas_call(
        functools.partial(_moe_finish_kernel, rows=rows),
        grid_spec=grid_spec,
        out_shape=jax.ShapeDtypeStruct((m, D_MODEL), F32),
        compiler_params=_params(("arbitrary",)),
        name="moe_finish",
    )(dest, y_sorted, h, g_final.reshape(1, D_MODEL))


def _moe_routed(hx, h, comb, wg_bf, wu_bf, wd_bf, g_final, *, rows):
    m = hx.shape[0]
    grp = comb[:, 0].astype(jnp.int32)
    onehot = (grp[:, None] == jnp.arange(N_GROUPS, dtype=jnp.int32)[None, :]).astype(jnp.int32)
    rank = jnp.cumsum(onehot, axis=0) - onehot
    counts = jnp.sum(onehot, axis=0)
    padded = ((counts + rows - 1) // rows) * rows
    seg_end = jnp.cumsum(padded)
    seg_start = seg_end - padded
    dest = jnp.sum(onehot * (seg_start[None, :] + rank), axis=1).astype(jnp.int32)
    p_rows = m + N_GROUPS * rows
    src = jnp.zeros((p_rows,), jnp.int32).at[dest].set(jnp.arange(m, dtype=jnp.int32))
    tile_start = jnp.arange(p_rows // rows, dtype=jnp.int32) * rows
    tile_group = jnp.minimum(jnp.sum((tile_start[:, None] >= seg_end[None, :]).astype(jnp.int32), axis=1),
                             N_GROUPS - 1).astype(jnp.int32)
    y_sorted = _moe_group(tile_group, src, hx, wg_bf, wu_bf, wd_bf, rows=rows)
    return _moe_finish(dest, y_sorted, h, g_final, rows=rows)


def _rope_tables(pos):
    half = HEAD_DIM // 2
    inv_freq = ROPE_THETA ** (-jnp.arange(half, dtype=F32) / half)
    ang = pos.astype(F32)[:, None] * inv_freq[None, :]
    cos, sin = jnp.cos(ang), jnp.sin(ang)
    return jnp.concatenate([cos, cos], axis=-1), jnp.concatenate([-sin, sin], axis=-1)


def _layer_tokens(x, pos_tables, init_state, attn_fn, wts, *, tm, n_seq, seq_len, tiles_per_seq, v_kind, routed):
    cos_t, sin_t = pos_tables
    xn = _rmsnorm(x, wts["norm_mix_g"], tm)
    (q,) = _proj_rope(xn, wts["w_in"], 0, cos_t, sin_t, tm=tm, rope=True, scale=SCALE * LOG2_E,
                      outs=(("tokens", BF16),), name="proj_q")
    k_f32, k_bf = _proj_rope(xn, wts["w_in"], 1, cos_t, sin_t, tm=tm, rope=True, scale=1.0,
                             outs=(("map_rows", F32), ("tokens", BF16)), name="proj_k")
    v_f32, v_bf = _proj_rope(xn, wts["w_in"], 2, cos_t, sin_t, tm=tm, rope=False, scale=1.0,
                             outs=(("tokens", F32), (v_kind, BF16)), name="proj_v")
    dw, new_conv = _proj_glu_conv(xn, wts["w_glu"], init_state, wts["w_dw"], wts["b_dw"],
                                  n_seq=n_seq, seq_len=seq_len, tiles_per_seq=tiles_per_seq)
    gates = _proj_gate(xn, wts["w_in"], 5, wts["b_gate"], tm=tm)
    o = attn_fn(q, k_bf, v_bf)
    ga = _attn_branch(o, wts["w_attn"], gates, tm=tm)
    merged = _conv_branch(dw, wts["conv_ln_g"], wts["conv_ln_b"], wts["w_conv"], gates, ga, tm=tm)
    h, hn, comb = _out_proj(merged, wts["w_out"], x, wts["norm_ffn_g"], wts["w_router"], wts["b_router"], tm=tm,
                            with_router_lanes=routed)
    moe = functools.partial(_moe_routed, rows=MOE_ROWS) if routed else functools.partial(_moe_dense, tm=tm)
    y = moe(hn, h, comb, wts["w_eg"], wts["w_eu"], wts["w_ed"], wts["norm_final_g"])
    return y, k_f32, v_f32, new_conv


def kernel(x_prompt, x_sample, cache_k, cache_v, state_conv, norm_mix_g, w_in, b_gate, lambda_q, lambda_k, subln_g, w_attn_branch, w_dw, b_dw, conv_ln_g, conv_ln_b, w_conv_branch, w_out, norm_ffn_g, w_router_group, b_router_group, w_router_expert, b_router_expert, w_exp_gate, w_exp_up, w_exp_down, norm_final_g):
    batch, seq, _ = x_prompt.shape
    n_streams, dec_seq, _ = x_sample.shape
    depth, _, past_len = cache_k.shape[:3]
    assert depth == 1
    assert past_len % CHUNK == 0 and dec_seq <= CHUNK
    l = 0
    lam_init = 0.8 - 0.6 * math.exp(-0.3 * l)

    pad_r = ROUTER_LANES - N_GROUPS - N_EXPERTS
    w_router = jnp.concatenate([w_router_group[l], w_router_expert[l], jnp.zeros((D_MODEL, pad_r), F32)], axis=1)
    b_router = jnp.concatenate([b_router_group[l], b_router_expert[l], jnp.zeros((pad_r,), F32)]).reshape(1, -1)
    w_router_hi = w_router.astype(BF16)
    w_router_lo = (w_router - w_router_hi.astype(F32)).astype(BF16)
    w_router = jnp.concatenate([w_router_hi, w_router_lo], axis=1)
    w_in_bf = w_in[l].astype(BF16)
    n_chunks = C_CONV // LANES
    glu_cols = N_HEADS * (4 * HEAD_DIM + V_DIM)
    w_val = w_in_bf[:, glu_cols:glu_cols + C_CONV].reshape(D_MODEL, n_chunks, 1, LANES)
    w_gat = w_in_bf[:, glu_cols + C_CONV:glu_cols + 2 * C_CONV].reshape(D_MODEL, n_chunks, 1, LANES)
    w_glu = jnp.concatenate([w_val, w_gat], axis=2).reshape(D_MODEL, n_chunks // GLU_CHUNKS, GLU_CHUNKS * 2 * LANES)
    w_glu = w_glu.transpose(1, 0, 2)
    wts = dict(
        norm_mix_g=norm_mix_g[l], w_in=w_in_bf, w_glu=w_glu, b_gate=b_gate[l],
        w_attn=w_attn_branch[l].astype(BF16), w_dw=w_dw[l], b_dw=b_dw[l],
        conv_ln_g=conv_ln_g[l], conv_ln_b=conv_ln_b[l], w_conv=w_conv_branch[l].astype(BF16),
        w_out=w_out[l].astype(BF16), norm_ffn_g=norm_ffn_g[l], w_router=w_router, b_router=b_router,
        w_eg=w_exp_gate[l].astype(BF16), w_eu=w_exp_up[l].astype(BF16), w_ed=w_exp_down[l].astype(BF16),
        norm_final_g=norm_final_g)
    lam_q, lam_k, sub_g = lambda_q[l], lambda_k[l], subln_g[l]

    tm = 512
    tabs_p = _rope_tables(jnp.arange(seq, dtype=jnp.int32))
    init_p = jnp.zeros((batch, HIST_ROWS, C_CONV), F32)
    attn_p = functools.partial(_flash_prompt, lam_q=lam_q, lam_k=lam_k, subln_g=sub_g,
                               batch=batch, seq=seq, tq=2 * tm, lam_init=lam_init)
    y_p, k_p, v_p, conv_p = _layer_tokens(
        x_prompt.reshape(batch * seq, D_MODEL), tabs_p, init_p, attn_p, wts,
        tm=tm, n_seq=1, seq_len=tm, tiles_per_seq=seq // tm, v_kind="head_transposed", routed=True)

    cos_s, sin_s = _rope_tables(past_len + jnp.arange(dec_seq, dtype=jnp.int32))
    tabs_s = (jnp.tile(cos_s, (n_streams, 1)), jnp.tile(sin_s, (n_streams, 1)))
    init_s = jnp.pad(state_conv[l], ((0, 0), (HIST_OFF, 0), (0, 0)))
    ck = cache_k[l].reshape(n_streams, past_len, D_MODEL)
    cv = cache_v[l].reshape(n_streams, past_len, D_MODEL)
    attn_s = functools.partial(_attn_sample, cache_k=ck, cache_v=cv, lam_q=lam_q, lam_k=lam_k, subln_g=sub_g,
                               n_streams=n_streams, dec_seq=dec_seq, past_len=past_len, lam_init=lam_init)
    m_s = n_streams * dec_seq
    y_s, k_s, v_s, conv_s = _layer_tokens(
        x_sample.reshape(m_s, D_MODEL), tabs_s, init_s, attn_s, wts,
        tm=m_s, n_seq=n_streams, seq_len=dec_seq, tiles_per_seq=1, v_kind="tokens", routed=False)

    return (y_p.reshape(batch, seq, D_MODEL),
            y_s.reshape(n_streams, dec_seq, D_MODEL),
            k_p.reshape(1, batch, seq, N_HEADS, 2, HEAD_DIM),
            v_p.reshape(1, batch, seq, N_HEADS, V_DIM),
            conv_p.reshape(1, batch, CONV_K - 1, C_CONV),
            k_s.reshape(1, n_streams, dec_seq, N_HEADS, 2, HEAD_DIM),
            v_s.reshape(1, n_streams, dec_seq, N_HEADS, V_DIM),
            conv_s.reshape(1, n_streams, CONV_K - 1, C_CONV))
```
